```python
import math
import jax, jax.numpy as jnp
from jax import lax
import numpy as np

D_MODEL = 4096
BATCH = 4
SEQ = 4096
DEPTH = 2

CHUNK = 64
Q_BLOCK = 128

MIX_WIDTH = D_MODEL
FOX_WIDTH = D_MODEL // 4
SSD_WIDTH = 3 * D_MODEL // 8
GDN_WIDTH = MIX_WIDTH - FOX_WIDTH - SSD_WIDTH

FOX_HEAD_DIM = 128
FOX_HEADS = FOX_WIDTH // FOX_HEAD_DIM

SSD_HEAD_DIM = 64
SSD_HEADS = SSD_WIDTH // SSD_HEAD_DIM
SSD_GROUPS = 4
SSD_HEADS_PER_GROUP = SSD_HEADS // SSD_GROUPS
SSD_STATE = 128
SSD_CONV_DIM = SSD_WIDTH + 2 * SSD_GROUPS * SSD_STATE

GDN_HEAD_DIM = 128
GDN_HEADS = GDN_WIDTH // GDN_HEAD_DIM

CONV_WIDTH = 4
EPS = 1e-6

IN_SIZES = (
    3 * FOX_WIDTH,
    FOX_HEADS,
    FOX_WIDTH,
    SSD_CONV_DIM,
    SSD_WIDTH,
    SSD_HEADS,
    3 * GDN_WIDTH,
    GDN_WIDTH,
    GDN_HEADS,
    GDN_HEADS,
)
IN_WIDTH = sum(IN_SIZES)
IN_SPLIT_POINTS = tuple(int(v) for v in np.cumsum(IN_SIZES)[:-1])

kernel_name = "hybrid_fox_ssd_gdn_parallel_heads"


def rms_norm(x, w):
    xf = x.astype(jnp.float32)
    y = xf * lax.rsqrt(jnp.mean(xf * xf, axis=-1, keepdims=True) + EPS)
    return (y * w.astype(jnp.float32)).astype(x.dtype)


def l2_norm(x):
    xf = x.astype(jnp.float32)
    return (xf * lax.rsqrt(jnp.sum(xf * xf, axis=-1, keepdims=True) + EPS)).astype(x.dtype)


def causal_depthwise_conv(x, w):
    k_width, channels = w.shape
    return lax.conv_general_dilated(
        x, w[:, None, :].astype(x.dtype), window_strides=(1,), padding=[(k_width - 1, 0)],
        dimension_numbers=("NWC", "WIO", "NWC"), feature_group_count=channels)


def fox_attention(q, k, v, log_f):
    bsz, seq, heads, dh = q.shape
    n_blocks = seq // Q_BLOCK
    F = jnp.cumsum(log_f.astype(jnp.float32), axis=1)
    Fk = F.transpose(0, 2, 1)
    kpos = jnp.arange(seq)
    scale = dh ** -0.5
    qb = q.reshape(bsz, n_blocks, Q_BLOCK, heads, dh).transpose(1, 0, 2, 3, 4)
    Fb = F.reshape(bsz, n_blocks, Q_BLOCK, heads).transpose(1, 0, 3, 2)
    qpos = kpos.reshape(n_blocks, Q_BLOCK)

    def block(args):
        qi, Fi, pi = args
        logits = jnp.einsum('bqhd,bkhd->bhqk', qi, k,
                            preferred_element_type=jnp.float32) * scale
        logits = logits + (Fi[..., :, None] - Fk[..., None, :])
        mask = pi[:, None] >= kpos[None, :]
        logits = jnp.where(mask, logits, -jnp.inf)
        p = jax.nn.softmax(logits, axis=-1)
        return jnp.einsum('bhqk,bkhd->bqhd', p.astype(v.dtype), v)

    o = lax.map(block, (qb, Fb, qpos))
    return o.transpose(1, 0, 2, 3, 4).reshape(bsz, seq, heads, dh)


def ssd_scan(x, dt, A, Bm, Cm):
    bsz, seq, heads, hp = x.shape
    groups, n_state = Bm.shape[2], Bm.shape[3]
    r = heads // groups
    nc = seq // CHUNK
    xdt = (x * dt[..., None]).reshape(bsz, nc, CHUNK, groups, r, hp)
    a = (dt * A).reshape(bsz, nc, CHUNK, groups, r)
    Bc = Bm.reshape(bsz, nc, CHUNK, groups, n_state)
    Cc = Cm.reshape(bsz, nc, CHUNK, groups, n_state)
    a_cs = jnp.cumsum(a, axis=2)
    tri = jnp.tril(jnp.ones((CHUNK, CHUNK), dtype=bool))
    seg = a_cs[:, :, :, None] - a_cs[:, :, None, :]
    lmat = jnp.exp(jnp.where(tri[:, :, None, None], seg, -jnp.inf))
    cb = jnp.einsum('bclgn,bcsgn->bclsg', Cc, Bc)
    y_diag = jnp.einsum('bclsgr,bcsgrp->bclgrp', cb[..., None] * lmat, xdt)
    decay_states = jnp.exp(a_cs[:, :, -1:] - a_cs)
    states = jnp.einsum('bclgn,bclgrp->bcgrpn', Bc, xdt * decay_states[..., None])
    chunk_decay = jnp.exp(a_cs[:, :, -1])

    def step(h, inp):
        st, dec = inp
        return h * dec[..., None, None] + st, h

    h0 = jnp.zeros_like(states[:, 0])
    _, h_in = lax.scan(step, h0, (jnp.moveaxis(states, 1, 0), jnp.moveaxis(chunk_decay, 1, 0)))
    h_in = jnp.moveaxis(h_in, 0, 1)
    y_off = jnp.einsum('bclgn,bcgrpn->bclgrp', Cc, h_in) * jnp.exp(a_cs)[..., None]
    return (y_diag + y_off).reshape(bsz, seq, heads, hp)


def gated_delta_rule(q, k, v, g, beta):
    out_dtype = v.dtype
    bsz, seq, heads, dk = q.shape
    dv = v.shape[-1]
    nc = seq // CHUNK

    def chunked(t):
        return t.astype(jnp.float32).reshape(bsz, nc, CHUNK, heads, -1).transpose(0, 3, 1, 2, 4)

    qc = chunked(q) * (dk ** -0.5)
    kc, vc = chunked(k), chunked(v)
    bc = beta.astype(jnp.float32).reshape(bsz, nc, CHUNK, heads).transpose(0, 3, 1, 2)
    gc = jnp.cumsum(g.astype(jnp.float32).reshape(bsz, nc, CHUNK, heads).transpose(0, 3, 1, 2), axis=-1)
    incl = jnp.tril(jnp.ones((CHUNK, CHUNK), dtype=bool))
    strict = jnp.tril(jnp.ones((CHUNK, CHUNK), dtype=bool), k=-1)
    diff = gc[..., :, None] - gc[..., None, :]
    decay = jnp.exp(jnp.where(incl, diff, -jnp.inf))
    k_beta = kc * bc[..., None]
    kk = jnp.einsum('bhcld,bhcsd->bhcls', k_beta, kc) * decay
    a_mat = jnp.where(strict, kk, 0.0) + jnp.eye(CHUNK, dtype=jnp.float32)
    rhs = jnp.concatenate([vc * bc[..., None], k_beta * jnp.exp(gc)[..., None]], axis=-1)
    sol = lax.linalg.triangular_solve(a_mat, rhs, left_side=True, lower=True, unit_diagonal=True)
    u, w = sol[..., :dv], sol[..., dv:]
    qk = jnp.einsum('bhcld,bhcsd->bhcls', qc, kc) * decay
    g_last = gc[..., -1]
    k_tail = kc * jnp.exp(g_last[..., None] - gc)[..., None]
    q_dec = qc * jnp.exp(gc)[..., None]

    def step(state, inp):
        u_i, w_i, qd_i, qk_i, kt_i, gl_i = inp
        v_new = u_i - jnp.einsum('bhlk,bhkv->bhlv', w_i, state)
        o_i = jnp.einsum('bhlk,bhkv->bhlv', qd_i, state) + jnp.einsum('bhls,bhsv->bhlv', qk_i, v_new)
        state = state * jnp.exp(gl_i)[..., None, None] + jnp.einsum('bhlk,bhlv->bhkv', kt_i, v_new)
        return state, o_i

    xs = (jnp.moveaxis(u, 2, 0), jnp.moveaxis(w, 2, 0), jnp.moveaxis(q_dec, 2, 0),
          jnp.moveaxis(qk, 2, 0), jnp.moveaxis(k_tail, 2, 0), jnp.moveaxis(g_last, 2, 0))
    s0 = jnp.zeros((bsz, heads, dk, dv), dtype=jnp.float32)
    _, o = lax.scan(step, s0, xs)
    return o.transpose(1, 0, 3, 2, 4).reshape(bsz, seq, heads, dv).astype(out_dtype)


def hybrid_layer(x, norm_w, w_in, w_out, fox_b_f, fox_q_norm_w, fox_k_norm_w, fox_out_norm_w,
                 ssd_conv_w, ssd_conv_b, ssd_dt_bias, ssd_A_log, ssd_D, ssd_norm_w,
                 gdn_conv_w, gdn_dt_bias, gdn_A_log, gdn_norm_w):
    bsz, seq, _ = x.shape
    h = rms_norm(x, norm_w)
    proj = jnp.einsum('bsd,de->bse', h, w_in)
    (fox_qkv, fox_f, fox_z, ssd_xbc, ssd_z, ssd_dt,
     gdn_qkv, gdn_z, gdn_beta, gdn_a) = jnp.split(proj, IN_SPLIT_POINTS, axis=-1)

    q, k, v = jnp.split(fox_qkv, 3, axis=-1)
    q = rms_norm(q.reshape(bsz, seq, FOX_HEADS, FOX_HEAD_DIM), fox_q_norm_w)
    k = rms_norm(k.reshape(bsz, seq, FOX_HEADS, FOX_HEAD_DIM), fox_k_norm_w)
    v = v.reshape(bsz, seq, FOX_HEADS, FOX_HEAD_DIM)
    log_f = jax.nn.log_sigmoid((fox_f + fox_b_f).astype(jnp.float32))
    o_fox = fox_attention(q, k, v, log_f)
    o_fox = rms_norm(o_fox, fox_out_norm_w).reshape(bsz, seq, FOX_WIDTH) * jax.nn.silu(fox_z)

    xbc = jax.nn.silu(causal_depthwise_conv(ssd_xbc, ssd_conv_w) + ssd_conv_b)
    xs, Bm, Cm = jnp.split(xbc, (SSD_WIDTH, SSD_WIDTH + SSD_GROUPS * SSD_STATE), axis=-1)
    xs = xs.reshape(bsz, seq, SSD_HEADS, SSD_HEAD_DIM)
    dt = jax.nn.softplus((ssd_dt + ssd_dt_bias).astype(jnp.float32))
    A = -jnp.exp(ssd_A_log.astype(jnp.float32))
    y = ssd_scan(xs, dt, A, Bm.reshape(bsz, seq, SSD_GROUPS, SSD_STATE),
                 Cm.reshape(bsz, seq, SSD_GROUPS, SSD_STATE))
    y = y + xs * ssd_D[:, None]
    y = y.reshape(bsz, seq, SSD_WIDTH) * jax.nn.silu(ssd_z)
    group_w = SSD_HEADS_PER_GROUP * SSD_HEAD_DIM
    o_ssd = rms_norm(y.reshape(bsz, seq, SSD_GROUPS, group_w),
                     ssd_norm_w.reshape(SSD_GROUPS, group_w)).reshape(bsz, seq, SSD_WIDTH)

    qkv = jax.nn.silu(causal_depthwise_conv(gdn_qkv, gdn_conv_w))
    gq, gk, gv = jnp.split(qkv, 3, axis=-1)
    gq = l2_norm(gq.reshape(bsz, seq, GDN_HEADS, GDN_HEAD_DIM))
    gk = l2_norm(gk.reshape(bsz, seq, GDN_HEADS, GDN_HEAD_DIM))
    gv = gv.reshape(bsz, seq, GDN_HEADS, GDN_HEAD_DIM)
    beta = jax.nn.sigmoid(gdn_beta.astype(jnp.float32))
    g = -jnp.exp(gdn_A_log.astype(jnp.float32)) * jax.nn.softplus((gdn_a + gdn_dt_bias).astype(jnp.float32))
    o_gdn = gated_delta_rule(gq, gk, gv, g, beta)
    o_gdn = rms_norm(o_gdn, gdn_norm_w).reshape(bsz, seq, GDN_WIDTH) * jax.nn.silu(gdn_z)

    mix = jnp.concatenate([o_fox.astype(x.dtype), o_ssd.astype(x.dtype), o_gdn.astype(x.dtype)], axis=-1)
    return x + jnp.einsum('bse,ed->bsd', mix, w_out)


def _inv_softplus_dt(key, shape):
    dt = jnp.exp(jax.random.uniform(key, shape, minval=math.log(1e-3), maxval=math.log(1e-1)))
    return dt + jnp.log(-jnp.expm1(-dt))


def setup_inputs(seed: int = 0) -> dict:
    key = jax.random.key(seed)
    ks = jax.random.split(key, 18)
    f32 = jnp.float32

    def gain(k, shape):
        return 1.0 + 0.02 * jax.random.normal(k, shape, f32)

    return {
        "x": jax.random.normal(ks[0], (BATCH, SEQ, D_MODEL), f32),
        "norm_w": gain(ks[1], (DEPTH, D_MODEL)),
        "w_in": jax.random.normal(ks[2], (DEPTH, D_MODEL, IN_WIDTH), f32) * D_MODEL ** -0.5,
        "w_out": jax.random.normal(ks[3], (DEPTH, MIX_WIDTH, D_MODEL), f32) * (0.5 * MIX_WIDTH ** -0.5),
        "fox_b_f": jax.random.uniform(ks[4], (DEPTH, FOX_HEADS), f32, minval=1.0, maxval=4.0),
        "fox_q_norm_w": gain(ks[5], (DEPTH, FOX_HEAD_DIM)),
        "fox_k_norm_w": gain(ks[6], (DEPTH, FOX_HEAD_DIM)),
        "fox_out_norm_w": gain(ks[7], (DEPTH, FOX_HEAD_DIM)),
        "ssd_conv_w": jax.random.normal(ks[8], (DEPTH, CONV_WIDTH, SSD_CONV_DIM), f32) * CONV_WIDTH ** -0.5,
        "ssd_conv_b": 0.02 * jax.random.normal(ks[9], (DEPTH, SSD_CONV_DIM), f32),
        "ssd_dt_bias": _inv_softplus_dt(ks[10], (DEPTH, SSD_HEADS)),
        "ssd_A_log": jnp.log(jax.random.uniform(ks[11], (DEPTH, SSD_HEADS), f32, minval=1.0, maxval=16.0)),
        "ssd_D": gain(ks[12], (DEPTH, SSD_HEADS)),
        "ssd_norm_w": gain(ks[13], (DEPTH, SSD_WIDTH)),
        "gdn_conv_w": jax.random.normal(ks[14], (DEPTH, CONV_WIDTH, 3 * GDN_WIDTH), f32) * CONV_WIDTH ** -0.5,
        "gdn_dt_bias": _inv_softplus_dt(ks[15], (DEPTH, GDN_HEADS)),
        "gdn_A_log": jnp.log(jax.random.uniform(ks[16], (DEPTH, GDN_HEADS), f32, minval=1.0, maxval=16.0)),
        "gdn_norm_w": gain(ks[17], (DEPTH, GDN_HEAD_DIM)),
    }


def reference(x, norm_w, w_in, w_out, fox_b_f, fox_q_norm_w, fox_k_norm_w, fox_out_norm_w,
              ssd_conv_w, ssd_conv_b, ssd_dt_bias, ssd_A_log, ssd_D, ssd_norm_w,
              gdn_conv_w, gdn_dt_bias, gdn_A_log, gdn_norm_w):
    for l in range(DEPTH):
        x = hybrid_layer(x, norm_w[l], w_in[l], w_out[l], fox_b_f[l], fox_q_norm_w[l],
                         fox_k_norm_w[l], fox_out_norm_w[l], ssd_conv_w[l], ssd_conv_b[l],
                         ssd_dt_bias[l], ssd_A_log[l], ssd_D[l], ssd_norm_w[l],
                         gdn_conv_w[l], gdn_dt_bias[l], gdn_A_log[l], gdn_norm_w[l])
    return x
```

```python
import functools

import jax
import jax.numpy as jnp
from jax import lax
from jax.experimental import pallas as pl
from jax.experimental.pallas import tpu as pltpu

F32 = jnp.float32
BF16 = jnp.bfloat16

EPS = 1e-6
CHUNK = 64
HEAD_DIM = 128
SSD_HEAD_DIM = 64
SSD_GROUPS = 4
SSD_STATE = 128
CONV_WIDTH = 4
CONV_HALO = 8
SSD_GATE_ROWS = 32
FOX_GATE_ROWS = 8
GDN_GATE_ROWS = 8
NEG_BIG = -1e30
VMEM_LIMIT_V7X = 56 * 1024 * 1024


def _dot(a, b):
    return jnp.dot(a, b, preferred_element_type=F32)


def _dot_nt(a, b):
    return lax.dot_general(a, b, (((1,), (1,)), ((), ())), preferred_element_type=F32)


def _dot_tn(a, b):
    return lax.dot_general(a, b, (((0,), (0,)), ((), ())), preferred_element_type=F32)


def _split3(a):
    a1 = a.astype(BF16)
    r1 = a - a1.astype(F32)
    a2 = r1.astype(BF16)
    r2 = r1 - a2.astype(F32)
    return a1, a2, r2.astype(BF16)


def _dot3(a, m):
    p1, p2, p3 = _split3(a)
    return _dot(p1, m) + _dot(p2, m) + _dot(p3, m)


def _dot3_nt(m, a):
    p1, p2, p3 = _split3(a)
    return _dot_nt(m, p1) + _dot_nt(m, p2) + _dot_nt(m, p3)


def _silu(x):
    return x * jax.nn.sigmoid(x)


def _softplus(x):
    return jnp.maximum(x, 0.0) + jnp.log1p(jnp.exp(-jnp.abs(x)))


def _log_sigmoid(x):
    return jnp.minimum(x, 0.0) - jnp.log1p(jnp.exp(-jnp.abs(x)))


def _iota2(shape):
    return (lax.broadcasted_iota(jnp.int32, shape, 0), lax.broadcasted_iota(jnp.int32, shape, 1))


def _pick_tile(n, candidates):
    for c in candidates:
        if n % c == 0 and n // c >= 2:
            return c
    return n


def _prenorm_kernel(x_ref, nw_ref, wgt_ref, h_ref, gt_ref):
    x = x_ref[...]
    ms = jnp.mean(x * x, axis=-1, keepdims=True)
    hb = (x * lax.rsqrt(ms + EPS) * nw_ref[...]).astype(BF16)
    h_ref[...] = hb
    gt_ref[...] = _dot_nt(wgt_ref[...], hb)


def _prenorm(x2d, norm_w, wgt):
    m, d = x2d.shape
    gr = wgt.shape[0]
    tm = _pick_tile(m, (512, 256, 128))
    return pl.pallas_call(
        _prenorm_kernel,
        out_shape=(jax.ShapeDtypeStruct((m, d), BF16), jax.ShapeDtypeStruct((gr, m), F32)),
        grid=(m // tm,),
        in_specs=[pl.BlockSpec((tm, d), lambda i: (i, 0)),
                  pl.BlockSpec((1, d), lambda i: (0, 0)),
                  pl.BlockSpec((gr, d), lambda i: (0, 0))],
        out_specs=(pl.BlockSpec((tm, d), lambda i: (i, 0)),
                   pl.BlockSpec((gr, tm), lambda i: (0, i))),
        compiler_params=pltpu.CompilerParams(dimension_semantics=("arbitrary",),
                                             vmem_limit_bytes=VMEM_LIMIT_V7X),
        name="prenorm_gates",
    )(x2d, norm_w.reshape(1, d), wgt)


def _in_proj_kernel(h_ref, w_ref, o_ref):
    o_ref[...] = _dot(h_ref[...], w_ref[...]).astype(o_ref.dtype)


def _in_proj(h, w):
    m, d = h.shape
    n = w.shape[1]
    tm = _pick_tile(m, (1024, 512, 256, 128))
    tn = _pick_tile(n, (1024, 512, 256, 128))
    return pl.pallas_call(
        _in_proj_kernel,
        out_shape=jax.ShapeDtypeStruct((m, n), BF16),
        grid=(m // tm, n // tn),
        in_specs=[pl.BlockSpec((tm, d), lambda i, j: (i, 0)),
                  pl.BlockSpec((d, tn), lambda i, j: (0, j))],
        out_specs=pl.BlockSpec((tm, tn), lambda i, j: (i, j)),
        compiler_params=pltpu.CompilerParams(dimension_semantics=("arbitrary", "arbitrary"),
                                             vmem_limit_bytes=VMEM_LIMIT_V7X),
        name="in_proj",
    )(h, w)


def _fox_gate_kernel(g_ref, b_ref, f_ref):
    seq = g_ref.shape[1]
    lf = _log_sigmoid(g_ref[...] + b_ref[...])
    r, c = _iota2((128, 128))
    upper = (r <= c).astype(BF16)
    carry = jnp.zeros((lf.shape[0], 1), F32)
    for i in range(seq // 128):
        cs = _dot3(lf[:, i * 128:(i + 1) * 128], upper) + carry
        f_ref[0, :, i * 128:(i + 1) * 128] = cs
        carry = cs[:, 127:128]


def _fox_gate(gt, b_f, bsz, seq, row_block):
    rows = FOX_GATE_ROWS
    return pl.pallas_call(
        _fox_gate_kernel,
        out_shape=jax.ShapeDtypeStruct((bsz, rows, seq), F32),
        grid=(bsz,),
        in_specs=[pl.BlockSpec((rows, seq), lambda b: (row_block, b)),
                  pl.BlockSpec((rows, 1), lambda b: (0, 0))],
        out_specs=pl.BlockSpec((1, rows, seq), lambda b: (b, 0, 0)),
        compiler_params=pltpu.CompilerParams(dimension_semantics=("arbitrary",)),
        name="fox_gate",
    )(gt, b_f)


def _fox_attn_kernel(q_ref, k_ref, v_ref, z_ref, f_ref, wq_ref, wk_ref, wo_ref, o_ref,
                     kn_ref, m_ref, l_ref, acc_ref, *, tq, scale):
    qi = pl.program_id(2)
    seq = k_ref.shape[0]

    @pl.when(qi == 0)
    def _():
        def body(i, carry):
            r0 = pl.multiple_of(i * tq, tq)
            kk = k_ref[pl.ds(r0, tq), :].astype(F32)
            ms = jnp.mean(kk * kk, axis=-1, keepdims=True)
            kn_ref[pl.ds(r0, tq), :] = (kk * lax.rsqrt(ms + EPS) * wk_ref[...]).astype(BF16)
            return carry
        lax.fori_loop(0, seq // tq, body, 0)

    q = q_ref[...].astype(F32)
    ms = jnp.mean(q * q, axis=-1, keepdims=True)
    qn = (q * lax.rsqrt(ms + EPS) * (wq_ref[...] * scale)).astype(BF16)
    q0 = pl.multiple_of(qi * tq, tq)
    f_first = f_ref[0, :, pl.ds(q0, tq)][:, 0:1]

    m_ref[...] = jnp.full(m_ref.shape, NEG_BIG, F32)
    l_ref[...] = jnp.zeros(l_ref.shape, F32)
    acc_ref[...] = jnp.zeros(acc_ref.shape, F32)

    def step(kb, masked):
        r0 = pl.multiple_of(kb * tq, tq)
        kblk = kn_ref[pl.ds(r0, tq), :]
        vblk = v_ref[pl.ds(r0, tq), :]
        fk = f_ref[0, :, pl.ds(r0, tq)]
        s = _dot_nt(qn, kblk) + (f_first - fk)
        if masked:
            r, c = _iota2((tq, tq))
            s = jnp.where(r >= c, s, NEG_BIG)
        m_prev = m_ref[...]
        m_new = jnp.maximum(m_prev, jnp.max(s, axis=-1, keepdims=True))
        p = jnp.exp(s - m_new)
        alpha = jnp.exp(m_prev - m_new)
        l_ref[...] = alpha * l_ref[...] + jnp.sum(p, axis=-1, keepdims=True)
        acc_ref[...] = alpha * acc_ref[...] + _dot(p.astype(BF16), vblk)
        m_ref[...] = m_new

    def loop_body(kb, carry):
        step(kb, False)
        return carry
    lax.fori_loop(0, qi, loop_body, 0)
    step(qi, True)

    o = acc_ref[...] / l_ref[...]
    ms = jnp.mean(o * o, axis=-1, keepdims=True)
    o = o * lax.rsqrt(ms + EPS) * wo_ref[...]
    o_ref[...] = (o * _silu(z_ref[...].astype(F32))).astype(o_ref.dtype)


def _fox_attn(proj, f3, wq, wk, wo, bsz, seq, heads, q_blk, k_blk, v_blk, z_blk):
    d = HEAD_DIM
    tq = _pick_tile(seq, (512, 256, 128))
    nq = seq // tq
    kern = functools.partial(_fox_attn_kernel, tq=tq, scale=d ** -0.5)
    vec = pl.BlockSpec((1, d), lambda b, h, i: (0, 0))
    return pl.pallas_call(
        kern,
        out_shape=jax.ShapeDtypeStruct((bsz * seq, heads * d), BF16),
        grid=(bsz, heads, nq),
        in_specs=[pl.BlockSpec((tq, d), lambda b, h, i: (b * nq + i, q_blk + h)),
                  pl.BlockSpec((seq, d), lambda b, h, i: (b, k_blk + h)),
                  pl.BlockSpec((seq, d), lambda b, h, i: (b, v_blk + h)),
                  pl.BlockSpec((tq, d), lambda b, h, i: (b * nq + i, z_blk + h)),
                  pl.BlockSpec((1, 1, seq), lambda b, h, i: (b * FOX_GATE_ROWS + h, 0, 0)),
                  vec, vec, vec],
        out_specs=pl.BlockSpec((tq, d), lambda b, h, i: (b * nq + i, h)),
        scratch_shapes=[pltpu.VMEM((seq, d), BF16),
                        pltpu.VMEM((tq, 1), F32),
                        pltpu.VMEM((tq, 1), F32),
                        pltpu.VMEM((tq, d), F32)],
        compiler_params=pltpu.CompilerParams(
            dimension_semantics=("arbitrary", "arbitrary", "arbitrary"),
            vmem_limit_bytes=VMEM_LIMIT_V7X),
        name="fox_attn",
    )(proj, proj, proj, proj, f3, wq.reshape(1, d), wk.reshape(1, d), wo.reshape(1, d))


def _conv_silu(buf, src_ref, w_ref, bias, tile):
    buf[CONV_HALO:CONV_HALO + tile, :] = src_ref[...].astype(F32)
    base = CONV_HALO - (CONV_WIDTH - 1)
    acc = w_ref[0:1, :] * buf[base:base + tile, :]
    for k in range(1, CONV_WIDTH):
        acc = acc + w_ref[k:k + 1, :] * buf[base + k:base + k + tile, :]
    if bias is not None:
        acc = acc + bias
    buf[0:CONV_HALO, :] = buf[tile:tile + CONV_HALO, :]
    return _silu(acc)


def _ssd_kernel(z_ref, xs_ref, b_ref, c_ref, g_ref, cwx_ref, cwb_ref, cwc_ref,
                cbx_ref, cbb_ref, cbc_ref, dtb_ref, alog_ref, dexp_ref, nw_ref, e_ref,
                o_ref, bufx, bufb, bufc, ht_ref, *, tile, heads):
    L = CHUNK
    P = SSD_HEAD_DIM
    N = SSD_STATE
    per_group = heads // SSD_GROUPS
    gw = per_group * P

    @pl.when(pl.program_id(1) == 0)
    def _():
        bufx[0:CONV_HALO, :] = jnp.zeros((CONV_HALO, bufx.shape[1]), F32)
        bufb[0:CONV_HALO, :] = jnp.zeros((CONV_HALO, bufb.shape[1]), F32)
        bufc[0:CONV_HALO, :] = jnp.zeros((CONV_HALO, bufc.shape[1]), F32)
        ht_ref[...] = jnp.zeros(ht_ref.shape, F32)

    xs = _conv_silu(bufx, xs_ref, cwx_ref, cbx_ref[...], tile)
    bm = _conv_silu(bufb, b_ref, cwb_ref, cbb_ref[...], tile)
    cm = _conv_silu(bufc, c_ref, cwc_ref, cbc_ref[...], tile)

    dt_row = _softplus(g_ref[...] + dtb_ref[...])
    a_row = dt_row * (-jnp.exp(alog_ref[...]))

    r, c = _iota2((L, L))
    tril = r >= c
    tri_u = (r <= c).astype(BF16)
    tri_l = tril.astype(BF16)
    eye = (r == c).astype(BF16)
    expand = e_ref[...]

    for ci in range(tile // L):
        sl = slice(ci * L, (ci + 1) * L)
        a_c = a_row[:, sl]
        cs_row = _dot3(a_c, tri_u)
        cs_col = _dot3_nt(tri_l, a_c)
        dt_col = _dot3_nt(eye, dt_row[:, sl])
        total = cs_col[L - 1:L, :]
        ds_col = jnp.exp(total - cs_col)
        ecs_col = jnp.exp(cs_col)
        cd_row = jnp.broadcast_to(jnp.exp(total), (8, total.shape[1]))
        stacked = jnp.concatenate([dt_col, ds_col, ecs_col, cd_row], axis=0)
        s1 = stacked.astype(BF16)
        s2 = (stacked - s1.astype(F32)).astype(BF16)
        ex = _dot(s1, expand) + _dot(s2, expand)
        dt_exp = ex[0:L]
        ds_exp = ex[L:2 * L]
        ecs_exp = ex[2 * L:3 * L]
        cd_exp = ex[3 * L:3 * L + 1]

        xs_c = xs[sl]
        xdt = xs_c * dt_exp
        xw = (xdt * ds_exp).astype(BF16)
        xdt_b = xdt.astype(BF16)
        y_groups = []
        for g in range(SSD_GROUPS):
            bg = bm[sl, g * N:(g + 1) * N]
            cg = cm[sl, g * N:(g + 1) * N].astype(BF16)
            cb = _dot_nt(cg, bg.astype(BF16))
            ht = ht_ref[g]
            cols = slice(g * gw, (g + 1) * gw)
            y_off = _dot(cg, ht.astype(BF16)) * ecs_exp[:, cols]
            ys = []
            for rr in range(per_group):
                hd = g * per_group + rr
                seg = cs_col[:, hd:hd + 1] - cs_row[hd:hd + 1, :]
                lm = jnp.exp(jnp.where(tril, seg, NEG_BIG))
                ys.append(_dot((cb * lm).astype(BF16), xdt_b[:, hd * P:(hd + 1) * P]))
            y_groups.append(jnp.concatenate(ys, axis=1) + y_off)
            ht_ref[g] = ht * cd_exp[:, cols] + _dot_tn(bg.astype(BF16), xw[:, cols])
        y = jnp.concatenate(y_groups, axis=1)
        y = (y + xs_c * dexp_ref[...]) * _silu(z_ref[sl, :].astype(F32))
        outs = []
        for g in range(SSD_GROUPS):
            yg = y[:, g * gw:(g + 1) * gw]
            ms = jnp.mean(yg * yg, axis=-1, keepdims=True)
            outs.append(yg * lax.rsqrt(ms + EPS))
        o_ref[sl, :] = (jnp.concatenate(outs, axis=1) * nw_ref[...]).astype(o_ref.dtype)


def _ssd(proj, gt, conv_w, conv_b, dt_bias, a_log, d_skip, norm_w, bsz, seq, heads,
         z_blk, xs_blk, b_blk, c_blk):
    width = heads * SSD_HEAD_DIM
    bcw = SSD_GROUPS * SSD_STATE
    tile = _pick_tile(seq, (256, 128, 64))
    nt = seq // tile
    rows = SSD_GATE_ROWS
    pad = rows - heads

    def col(v):
        return jnp.pad(v.astype(F32), (0, pad)).reshape(rows, 1)

    head_of_col = jnp.arange(width) // SSD_HEAD_DIM
    expand = (jnp.arange(rows)[:, None] == head_of_col[None, :]).astype(BF16)
    d_exp = jnp.repeat(d_skip.astype(F32), SSD_HEAD_DIM).reshape(1, width)
    cwx, cwb, cwc = conv_w[:, :width], conv_w[:, width:width + bcw], conv_w[:, width + bcw:]
    cbx = conv_b[:width].reshape(1, width)
    cbb = conv_b[width:width + bcw].reshape(1, bcw)
    cbc = conv_b[width + bcw:].reshape(1, bcw)

    def full(shape):
        return pl.BlockSpec(shape, lambda b, t: (0,) * len(shape))

    kern = functools.partial(_ssd_kernel, tile=tile, heads=heads)
    return pl.pallas_call(
        kern,
        out_shape=jax.ShapeDtypeStruct((bsz * seq, width), BF16),
        grid=(bsz, nt),
        in_specs=[pl.BlockSpec((tile, width), lambda b, t: (b * nt + t, z_blk)),
                  pl.BlockSpec((tile, width), lambda b, t: (b * nt + t, xs_blk)),
                  pl.BlockSpec((tile, bcw), lambda b, t: (b * nt + t, b_blk)),
                  pl.BlockSpec((tile, bcw), lambda b, t: (b * nt + t, c_blk)),
                  pl.BlockSpec((rows, tile), lambda b, t: (0, b * nt + t)),
                  full((CONV_WIDTH, width)), full((CONV_WIDTH, bcw)), full((CONV_WIDTH, bcw)),
                  full((1, width)), full((1, bcw)), full((1, bcw)),
                  full((rows, 1)), full((rows, 1)), full((1, width)), full((1, width)),
                  full((rows, width))],
        out_specs=pl.BlockSpec((tile, width), lambda b, t: (b * nt + t, 0)),
        scratch_shapes=[pltpu.VMEM((tile + CONV_HALO, width), F32),
                        pltpu.VMEM((tile + CONV_HALO, bcw), F32),
                        pltpu.VMEM((tile + CONV_HALO, bcw), F32),
                        pltpu.VMEM((SSD_GROUPS, SSD_STATE, width // SSD_GROUPS), F32)],
        compiler_params=pltpu.CompilerParams(dimension_semantics=("arbitrary", "arbitrary"),
                                             vmem_limit_bytes=VMEM_LIMIT_V7X),
        name="ssd",
    )(proj, proj, proj, proj, gt, cwx, cwb, cwc, cbx, cbb, cbc, col(dt_bias), col(a_log),
      d_exp, norm_w.reshape(1, width), expand)


def _unit_lower_inverse(n_strict, r, c):
    eye = (r == c).astype(F32)
    same2 = (r >> 1) == (c >> 1)
    t = eye - jnp.where(same2, n_strict, 0.0)
    s = 2
    while s < CHUNK:
        sh = s.bit_length()
        in_block = (r >> sh) == (c >> sh)
        low_left = in_block & ((r & (2 * s - 1)) >= s) & ((c & (2 * s - 1)) < s)
        cpart = jnp.where(low_left, n_strict, 0.0).astype(BF16)
        tb = t.astype(BF16)
        t = t - _dot(tb, _dot(cpart, tb).astype(BF16))
        s *= 2
    return t


def _gdn_kernel(q_ref, k_ref, v_ref, z_ref, g_ref, cwq_ref, cwk_ref, cwv_ref, dtb_ref, alog_ref,
                nw_ref, o_ref, bufq, bufk, bufv, s_ref, *, tile, hg, scale):
    L = CHUNK
    D = HEAD_DIM

    @pl.when(pl.program_id(2) == 0)
    def _():
        bufq[0:CONV_HALO, :] = jnp.zeros((CONV_HALO, bufq.shape[1]), F32)
        bufk[0:CONV_HALO, :] = jnp.zeros((CONV_HALO, bufk.shape[1]), F32)
        bufv[0:CONV_HALO, :] = jnp.zeros((CONV_HALO, bufv.shape[1]), F32)
        s_ref[...] = jnp.zeros(s_ref.shape, F32)

    q_all = _conv_silu(bufq, q_ref, cwq_ref, None, tile)
    k_all = _conv_silu(bufk, k_ref, cwk_ref, None, tile)
    v_all = _conv_silu(bufv, v_ref, cwv_ref, None, tile)

    gl = g_ref[...]
    rows8 = lax.broadcasted_iota(jnp.int32, gl.shape, 0)
    decay_log = -jnp.exp(alog_ref[0]) * _softplus(gl + dtb_ref[0])
    gates = jnp.where(rows8 < 4, decay_log, jax.nn.sigmoid(gl))
    gates = jnp.concatenate([gates, jnp.zeros_like(gates)], axis=0)

    r, c = _iota2((L, L))
    incl = r >= c
    strict = r > c
    tri_u = (r <= c).astype(BF16)
    tri_eye = jnp.concatenate([incl.astype(BF16), (r == c).astype(BF16)], axis=0)

    for h in range(hg):
        hs = slice(h * D, (h + 1) * D)
        qh = q_all[:, hs]
        kh = k_all[:, hs]
        qh = qh * (lax.rsqrt(jnp.sum(qh * qh, axis=-1, keepdims=True) + EPS) * scale)
        kh = kh * lax.rsqrt(jnp.sum(kh * kh, axis=-1, keepdims=True) + EPS)
        vh = v_all[:, hs]
        zh = z_ref[:, hs].astype(F32)
        for ci in range(tile // L):
            sl = slice(ci * L, (ci + 1) * L)
            gc_all = gates[:, sl]
            cols = _dot3_nt(tri_eye, gc_all)
            gc_row = _dot3(gc_all, tri_u)
            gcc = cols[0:L, h:h + 1]
            gcr = gc_row[h:h + 1, :]
            beta = cols[L:2 * L, 4 + h:5 + h]
            g_last = gcc[L - 1:L, :]
            decay = jnp.exp(jnp.where(incl, gcc - gcr, NEG_BIG))

            kc = kh[sl]
            qc = qh[sl]
            vc = vh[sl]
            kc_b = kc.astype(BF16)
            k_beta = kc * beta
            kk = _dot_nt(k_beta.astype(BF16), kc_b) * decay
            t_inv = _unit_lower_inverse(jnp.where(strict, kk, 0.0), r, c)
            rhs = jnp.concatenate([vc * beta, k_beta * jnp.exp(gcc)], axis=1)
            sol = _dot(t_inv.astype(BF16), rhs.astype(BF16))
            u = sol[:, 0:D]
            w = sol[:, D:2 * D]
            qk = _dot_nt(qc.astype(BF16), kc_b) * decay
            k_tail = kc * jnp.exp(g_last - gcc)
            q_dec = qc * jnp.exp(gcc)

            state = s_ref[h]
            state_b = state.astype(BF16)
            v_new = u - _dot(w.astype(BF16), state_b)
            v_new_b = v_new.astype(BF16)
            o = _dot(q_dec.astype(BF16), state_b) + _dot(qk.astype(BF16), v_new_b)
            s_ref[h] = state * jnp.exp(g_last) + _dot_tn(k_tail.astype(BF16), v_new_b)

            ms = jnp.mean(o * o, axis=-1, keepdims=True)
            o = o * lax.rsqrt(ms + EPS) * nw_ref[...]
            o_ref[sl, hs] = (o * _silu(zh[sl])).astype(o_ref.dtype)


def _gdn(proj, gt, conv_w, dt_bias, a_log, norm_w, bsz, seq, heads, hg, q_blk, k_blk, v_blk,
         z_blk, gate_blk):
    d = HEAD_DIM
    width = heads * d
    gwid = hg * d
    ngrp = heads // hg
    tile = _pick_tile(seq, (128, 64))
    nt = seq // tile
    cwq, cwk, cwv = conv_w[:, :width], conv_w[:, width:2 * width], conv_w[:, 2 * width:]

    def grp_col(v):
        v = jnp.pad(v.astype(F32).reshape(ngrp, hg), ((0, 0), (0, GDN_GATE_ROWS - hg)))
        return v.reshape(ngrp, GDN_GATE_ROWS, 1)

    def tok(blk):
        return pl.BlockSpec((tile, gwid), lambda b, g, t: (b * nt + t, blk + g))

    def cw():
        return pl.BlockSpec((CONV_WIDTH, gwid), lambda b, g, t: (0, g))

    def gcol():
        return pl.BlockSpec((1, GDN_GATE_ROWS, 1), lambda b, g, t: (g, 0, 0))

    kern = functools.partial(_gdn_kernel, tile=tile, hg=hg, scale=d ** -0.5)
    return pl.pallas_call(
        kern,
        out_shape=jax.ShapeDtypeStruct((bsz * seq, width), BF16),
        grid=(bsz, ngrp, nt),
        in_specs=[tok(q_blk), tok(k_blk), tok(v_blk), tok(z_blk),
                  pl.BlockSpec((GDN_GATE_ROWS, tile), lambda b, g, t: (gate_blk + g, b * nt + t)),
                  cw(), cw(), cw(), gcol(), gcol(),
                  pl.BlockSpec((1, d), lambda b, g, t: (0, 0))],
        out_specs=pl.BlockSpec((tile, gwid), lambda b, g, t: (b * nt + t, g)),
        scratch_shapes=[pltpu.VMEM((tile + CONV_HALO, gwid), F32),
                        pltpu.VMEM((tile + CONV_HALO, gwid), F32),
                        pltpu.VMEM((tile + CONV_HALO, gwid), F32),
                        pltpu.VMEM((hg, d, d), F32)],
        compiler_params=pltpu.CompilerParams(
            dimension_semantics=("arbitrary", "arbitrary", "arbitrary"),
            vmem_limit_bytes=VMEM_LIMIT_V7X),
        name="gdn",
    )(proj, proj, proj, proj, gt, cwq, cwk, cwv, grp_col(dt_bias), grp_col(a_log),
      norm_w.reshape(1, d))


def _out_proj_kernel(x_ref, mf_ref, ms_ref, mg_ref, wf_ref, ws_ref, wg_ref, o_ref):
    acc = _dot(mf_ref[...], wf_ref[...])
    acc = acc + _dot(ms_ref[...], ws_ref[...])
    acc = acc + _dot(mg_ref[...], wg_ref[...])
    o_ref[...] = x_ref[...] + acc


def _out_proj(x2d, mf, ms, mg, wf, ws, wg):
    m, d = x2d.shape
    tm = _pick_tile(m, (1024, 512, 256, 128))
    tn = _pick_tile(d, (1024, 512, 256, 128))

    def act(a):
        return pl.BlockSpec((tm, a.shape[1]), lambda i, j: (i, 0))

    def wgt(a):
        return pl.BlockSpec((a.shape[0], tn), lambda i, j: (0, j))

    return pl.pallas_call(
        _out_proj_kernel,
        out_shape=jax.ShapeDtypeStruct((m, d), F32),
        grid=(m // tm, d // tn),
        in_specs=[pl.BlockSpec((tm, tn), lambda i, j: (i, j)),
                  act(mf), act(ms), act(mg), wgt(wf), wgt(ws), wgt(wg)],
        out_specs=pl.BlockSpec((tm, tn), lambda i, j: (i, j)),
        compiler_params=pltpu.CompilerParams(dimension_semantics=("arbitrary", "arbitrary"),
                                             vmem_limit_bytes=VMEM_LIMIT_V7X),
        name="out_proj",
    )(x2d, mf, ms, mg, wf, ws, wg)


def _layer(x2d, bsz, seq, norm_w, w_in, w_out, fox_b_f, fox_q_norm_w, fox_k_norm_w,
           fox_out_norm_w, ssd_conv_w, ssd_conv_b, ssd_dt_bias, ssd_a_log, ssd_d, ssd_norm_w,
           gdn_conv_w, gdn_dt_bias, gdn_a_log, gdn_norm_w):
    d_model = x2d.shape[1]
    fox_w = d_model // 4
    ssd_w = 3 * d_model // 8
    gdn_w = d_model - fox_w - ssd_w
    fox_h = fox_w // HEAD_DIM
    ssd_h = ssd_w // SSD_HEAD_DIM
    gdn_h = gdn_w // HEAD_DIM
    bcw = SSD_GROUPS * SSD_STATE
    hg = 4 if gdn_h % 4 == 0 else 2
    ngrp = gdn_h // hg
    assert fox_h <= FOX_GATE_ROWS and ssd_h <= SSD_GATE_ROWS and ssd_h % SSD_GROUPS == 0
    assert gdn_h % hg == 0 and seq % CHUNK == 0

    sizes = (3 * fox_w, fox_h, fox_w, ssd_w + 2 * bcw, ssd_w, ssd_h, 3 * gdn_w, gdn_w, gdn_h, gdn_h)
    offs = [0]
    for s in sizes:
        offs.append(offs[-1] + s)
    (o_fqkv, o_ff, o_fz, o_sxbc, o_sz, o_sdt, o_gqkv, o_gz, o_gb, o_ga) = offs[:-1]

    def cols(o, n):
        return w_in[:, o:o + n]

    w_main = jnp.concatenate(
        [cols(o_sz, ssd_w), cols(o_sxbc, ssd_w + 2 * bcw), cols(o_fqkv, 3 * fox_w),
         cols(o_fz, fox_w), cols(o_gqkv, 3 * gdn_w), cols(o_gz, gdn_w)], axis=1).astype(BF16)
    c_sz, c_sxs, c_sb = 0, ssd_w, 2 * ssd_w
    c_sc = c_sb + bcw
    c_fq = c_sc + bcw
    c_fz = c_fq + 3 * fox_w
    c_gq = c_fz + fox_w
    c_gz = c_gq + 3 * gdn_w
    gwid = hg * HEAD_DIM
    assert c_sb % bcw == 0 and c_fq % HEAD_DIM == 0 and c_gq % gwid == 0 and gdn_w % gwid == 0

    def rows(o, n, total):
        return jnp.pad(w_in[:, o:o + n].T, ((0, total - n), (0, 0)))

    gate_rows = [rows(o_sdt, ssd_h, SSD_GATE_ROWS), rows(o_ff, fox_h, FOX_GATE_ROWS)]
    for g in range(ngrp):
        gate_rows.append(rows(o_ga + g * hg, hg, 4))
        gate_rows.append(rows(o_gb + g * hg, hg, 4))
    wgt = jnp.concatenate(gate_rows, axis=0).astype(BF16)
    fox_row_blk = SSD_GATE_ROWS // FOX_GATE_ROWS
    gdn_row_blk = (SSD_GATE_ROWS + FOX_GATE_ROWS) // GDN_GATE_ROWS

    h, gt = _prenorm(x2d, norm_w, wgt)
    proj = _in_proj(h, w_main)

    b_f = jnp.pad(fox_b_f.astype(F32), (0, FOX_GATE_ROWS - fox_h)).reshape(FOX_GATE_ROWS, 1)
    f_cum = _fox_gate(gt, b_f, bsz, seq, fox_row_blk)
    f3 = f_cum.reshape(bsz * FOX_GATE_ROWS, 1, seq)
    mix_fox = _fox_attn(proj, f3, fox_q_norm_w, fox_k_norm_w, fox_out_norm_w, bsz, seq, fox_h,
                        c_fq // HEAD_DIM, (c_fq + fox_w) // HEAD_DIM,
                        (c_fq + 2 * fox_w) // HEAD_DIM, c_fz // HEAD_DIM)
    mix_ssd = _ssd(proj, gt, ssd_conv_w, ssd_conv_b, ssd_dt_bias, ssd_a_log, ssd_d, ssd_norm_w,
                   bsz, seq, ssd_h, c_sz // ssd_w, c_sxs // ssd_w, c_sb // bcw, c_sc // bcw)
    mix_gdn = _gdn(proj, gt, gdn_conv_w, gdn_dt_bias, gdn_a_log, gdn_norm_w, bsz, seq, gdn_h, hg,
                   c_gq // gwid, (c_gq + gdn_w) // gwid, (c_gq + 2 * gdn_w) // gwid,
                   c_gz // gwid, gdn_row_blk)

    w_o = w_out.astype(BF16)
    return _out_proj(x2d, mix_fox, mix_ssd, mix_gdn, w_o[:fox_w], w_o[fox_w:fox_w + ssd_w],
                     w_o[fox_w + ssd_w:])


def kernel(x, norm_w, w_in, w_out, fox_b_f, fox_q_norm_w, fox_k_norm_w, fox_out_norm_w, ssd_conv_w, ssd_conv_b, ssd_dt_bias, ssd_A_log, ssd_D, ssd_norm_w, gdn_conv_w, gdn_dt_bias, gdn_A_log, gdn_norm_w):
    bsz, seq, d_model = x.shape
    x2d = x.reshape(bsz * seq, d_model)
    for l in range(norm_w.shape[0]):
        x2d = _layer(x2d, bsz, seq, norm_w[l], w_in[l], w_out[l], fox_b_f[l], fox_q_norm_w[l],
                     fox_k_norm_w[l], fox_out_norm_w[l], ssd_conv_w[l], ssd_conv_b[l],
                     ssd_dt_bias[l], ssd_A_log[l], ssd_D[l], ssd_norm_w[l], gdn_conv_w[l],
                     gdn_dt_bias[l], gdn_A_log[l], gdn_norm_w[l])
    return x2d.reshape(bsz, seq, d_model)
```

```python
import functools

import jax
import jax.numpy as jnp
from jax import lax
from jax.experimental import pallas as pl
from jax.experimental.pallas import tpu as pltpu

F32 = jnp.float32
BF16 = jnp.bfloat16

EPS = 1e-6
CHUNK = 64
HEAD_DIM = 128
SSD_HEAD_DIM = 64
SSD_GROUPS = 4
SSD_STATE = 128
CONV_WIDTH = 4
CONV_HALO = 8
SSD_GATE_ROWS = 32
FOX_GATE_ROWS = 8
GDN_GATE_ROWS = 8
NEG_BIG = -1e30
VMEM_LIMIT_V7X = 56 * 1024 * 1024


def _dot(a, b):
    return jnp.dot(a, b, preferred_element_type=F32)


def _dot_nt(a, b):
    return lax.dot_general(a, b, (((1,), (1,)), ((), ())), preferred_element_type=F32)


def _dot_tn(a, b):
    return lax.dot_general(a, b, (((0,), (0,)), ((), ())), preferred_element_type=F32)


def _split3(a):
    a1 = a.astype(BF16)
    r1 = a - a1.astype(F32)
    a2 = r1.astype(BF16)
    r2 = r1 - a2.astype(F32)
    return a1, a2, r2.astype(BF16)


def _dot3(a, m):
    p1, p2, p3 = _split3(a)
    return _dot(p1, m) + _dot(p2, m) + _dot(p3, m)


def _dot3_nt(m, a):
    p1, p2, p3 = _split3(a)
    return _dot_nt(m, p1) + _dot_nt(m, p2) + _dot_nt(m, p3)


def _silu(x):
    return x * jax.nn.sigmoid(x)


def _softplus(x):
    return jnp.maximum(x, 0.0) + jnp.log1p(jnp.exp(-jnp.abs(x)))


def _log_sigmoid(x):
    return jnp.minimum(x, 0.0) - jnp.log1p(jnp.exp(-jnp.abs(x)))


def _iota2(shape):
    return (lax.broadcasted_iota(jnp.int32, shape, 0), lax.broadcasted_iota(jnp.int32, shape, 1))


def _pick_tile(n, candidates):
    for c in candidates:
        if n % c == 0 and n // c >= 2:
            return c
    return n


def _prenorm_kernel(x_ref, nw_ref, wgt_ref, h_ref, gt_ref):
    x = x_ref[...]
    ms = jnp.mean(x * x, axis=-1, keepdims=True)
    hb = (x * lax.rsqrt(ms + EPS) * nw_ref[...]).astype(BF16)
    h_ref[...] = hb
    gt_ref[...] = _dot_nt(wgt_ref[...], hb)


def _prenorm(x2d, norm_w, wgt):
    m, d = x2d.shape
    gr = wgt.shape[0]
    tm = _pick_tile(m, (512, 256, 128))
    return pl.pallas_call(
        _prenorm_kernel,
        out_shape=(jax.ShapeDtypeStruct((m, d), BF16), jax.ShapeDtypeStruct((gr, m), F32)),
        grid=(m // tm,),
        in_specs=[pl.BlockSpec((tm, d), lambda i: (i, 0)),
                  pl.BlockSpec((1, d), lambda i: (0, 0)),
                  pl.BlockSpec((gr, d), lambda i: (0, 0))],
        out_specs=(pl.BlockSpec((tm, d), lambda i: (i, 0)),
                   pl.BlockSpec((gr, tm), lambda i: (0, i))),
        compiler_params=pltpu.CompilerParams(dimension_semantics=("arbitrary",),
                                             vmem_limit_bytes=VMEM_LIMIT_V7X),
        name="prenorm_gates",
    )(x2d, norm_w.reshape(1, d), wgt)


def _in_proj_kernel(h_ref, w_ref, o_ref):
    o_ref[...] = _dot(h_ref[...], w_ref[...]).astype(o_ref.dtype)


def _in_proj(h, w):
    m, d = h.shape
    n = w.shape[1]
    tm = _pick_tile(m, (1024, 512, 256, 128))
    tn = _pick_tile(n, (1024, 512, 256, 128))
    return pl.pallas_call(
        _in_proj_kernel,
        out_shape=jax.ShapeDtypeStruct((m, n), BF16),
        grid=(m // tm, n // tn),
        in_specs=[pl.BlockSpec((tm, d), lambda i, j: (i, 0)),
                  pl.BlockSpec((d, tn), lambda i, j: (0, j))],
        out_specs=pl.BlockSpec((tm, tn), lambda i, j: (i, j)),
        compiler_params=pltpu.CompilerParams(dimension_semantics=("arbitrary", "arbitrary"),
                                             vmem_limit_bytes=VMEM_LIMIT_V7X),
        name="in_proj",
    )(h, w)


def _fox_gate_kernel(g_ref, b_ref, f_ref):
    seq = g_ref.shape[1]
    lf = _log_sigmoid(g_ref[...] + b_ref[...])
    r, c = _iota2((128, 128))
    upper = (r <= c).astype(BF16)
    carry = jnp.zeros((lf.shape[0], 1), F32)
    for i in range(seq // 128):
        cs = _dot3(lf[:, i * 128:(i + 1) * 128], upper) + carry
        f_ref[0, :, i * 128:(i + 1) * 128] = cs
        carry = cs[:, 127:128]


def _fox_gate(gt, b_f, bsz, seq, row_block):
    rows = FOX_GATE_ROWS
    return pl.pallas_call(
        _fox_gate_kernel,
        out_shape=jax.ShapeDtypeStruct((bsz, rows, seq), F32),
        grid=(bsz,),
        in_specs=[pl.BlockSpec((rows, seq), lambda b: (row_block, b)),
                  pl.BlockSpec((rows, 1), lambda b: (0, 0))],
        out_specs=pl.BlockSpec((1, rows, seq), lambda b: (b, 0, 0)),
        compiler_params=pltpu.CompilerParams(dimension_semantics=("arbitrary",)),
        name="fox_gate",
    )(gt, b_f)


LOG2E = 1.4426950408889634
FOX_SPLIT = 2
FOX_VROWS = 144


def _fox_attn_kernel(q_ref, k_ref, v_ref, z_ref, f_ref, wq_ref, wk_ref, wo_ref, o_ref,
                     ka_ref, vat_ref, qa_ref, m_ref, acc_ref, *, tq, scale):
    qi = pl.program_id(2)
    seq = k_ref.shape[0]
    d = HEAD_DIM
    tw = tq // FOX_SPLIT
    lane = lax.broadcasted_iota(jnp.int32, (tq, d), 1)

    @pl.when(qi == 0)
    def _():
        r, c = _iota2((tq, tq))
        eye = (r == c).astype(BF16)
        row8 = lax.broadcasted_iota(jnp.int32, (8, tq), 0)
        ones_row = (lax.broadcasted_iota(jnp.int32, (FOX_VROWS - d, tq), 0) == 0).astype(BF16)

        def body(i, carry):
            r0 = pl.multiple_of(i * tq, tq)
            kk = k_ref[pl.ds(r0, tq), :].astype(F32)
            ms = jnp.mean(kk * kk, axis=-1, keepdims=True)
            ka_ref[pl.ds(r0, tq), 0:d] = (kk * lax.rsqrt(ms + EPS) * wk_ref[...]).astype(BF16)
            frow = f_ref[0, :, pl.ds(r0, tq)]
            b1, b2, b3 = _split3((frow[:, 0:1] - frow) * LOG2E)
            parts = jnp.where(row8 == 0, b1.astype(F32),
                              jnp.where(row8 == 1, b2.astype(F32),
                                        jnp.where(row8 == 2, b3.astype(F32), 0.0)))
            parts = jnp.concatenate([parts, jnp.zeros((d - 8, tq), F32)], axis=0).astype(BF16)
            ka_ref[pl.ds(r0, tq), d:2 * d] = _dot_nt(eye, parts).astype(BF16)
            vt = jnp.transpose(v_ref[pl.ds(r0, tq), :].astype(F32))
            vat_ref[i, 0:d, :] = vt.astype(BF16)
            vat_ref[i, d:FOX_VROWS, :] = ones_row
            return carry
        lax.fori_loop(0, seq // tq, body, 0)

    q = q_ref[...].astype(F32)
    ms = jnp.mean(q * q, axis=-1, keepdims=True)
    qa_ref[:, 0:d] = (q * lax.rsqrt(ms + EPS) * (wq_ref[...] * (scale * LOG2E))).astype(BF16)
    qa_ref[:, d:2 * d] = (lane < 3).astype(BF16)
    q0 = pl.multiple_of(qi * tq, tq)
    f_first = f_ref[0, :, pl.ds(q0, tq)][:, 0:1]

    m_ref[...] = jnp.full(m_ref.shape, NEG_BIG, F32)
    acc_ref[...] = jnp.zeros(acc_ref.shape, F32)

    def step(kb, masked):
        r0 = pl.multiple_of(kb * tq, tq)
        shift = (f_first - f_ref[0, :, pl.ds(r0, tq)][:, 0:1]) * LOG2E
        nks = [(h + 1) * tw if masked else tq for h in range(FOX_SPLIT)]
        scores = [_dot_nt(ka_ref[pl.ds(r0, nks[h]), :], qa_ref[h * tw:(h + 1) * tw, :])
                  for h in range(FOX_SPLIT)]
        new_m, new_acc = [], []
        for h in range(FOX_SPLIT):
            cols = slice(h * tw, (h + 1) * tw)
            s = scores[h]
            if masked:
                r, c = _iota2((nks[h], tw))
                s = jnp.where(r <= c + h * tw, s, NEG_BIG)
            m_prev = m_ref[:, cols]
            m_new = jnp.maximum(m_prev, jnp.max(s, axis=0, keepdims=True) + shift)
            p = jnp.exp2(s - (m_new - shift)).astype(BF16)
            alpha = jnp.exp2(m_prev - m_new)
            pv = _dot(vat_ref[kb, :, 0:nks[h]], p)
            new_acc.append(alpha * acc_ref[:, cols] + pv)
            new_m.append(m_new)
        m_ref[...] = jnp.concatenate(new_m, axis=1)
        acc_ref[...] = jnp.concatenate(new_acc, axis=1)

    def loop_body(kb, carry):
        step(kb, False)
        return carry
    lax.fori_loop(0, qi, loop_body, 0)
    step(qi, True)

    o = jnp.transpose(acc_ref[0:d, :] / acc_ref[d:d + 1, :])
    ms = jnp.mean(o * o, axis=-1, keepdims=True)
    o = o * lax.rsqrt(ms + EPS) * wo_ref[...]
    o_ref[...] = (o * _silu(z_ref[...].astype(F32))).astype(o_ref.dtype)


def _fox_attn(proj, f3, wq, wk, wo, bsz, seq, heads, q_blk, k_blk, v_blk, z_blk):
    d = HEAD_DIM
    tq = _pick_tile(seq, (512, 256, 128))
    nq = seq // tq
    kern = functools.partial(_fox_attn_kernel, tq=tq, scale=d ** -0.5)
    vec = pl.BlockSpec((1, d), lambda b, h, i: (0, 0))
    return pl.pallas_call(
        kern,
        out_shape=jax.ShapeDtypeStruct((bsz * seq, heads * d), BF16),
        grid=(bsz, heads, nq),
        in_specs=[pl.BlockSpec((tq, d), lambda b, h, i: (b * nq + i, q_blk + h)),
                  pl.BlockSpec((seq, d), lambda b, h, i: (b, k_blk + h)),
                  pl.BlockSpec((seq, d), lambda b, h, i: (b, v_blk + h)),
                  pl.BlockSpec((tq, d), lambda b, h, i: (b * nq + i, z_blk + h)),
                  pl.BlockSpec((1, 1, seq), lambda b, h, i: (b * FOX_GATE_ROWS + h, 0, 0)),
                  vec, vec, vec],
        out_specs=pl.BlockSpec((tq, d), lambda b, h, i: (b * nq + i, h)),
        scratch_shapes=[pltpu.VMEM((seq, 2 * d), BF16),
                        pltpu.VMEM((nq, FOX_VROWS, tq), BF16),
                        pltpu.VMEM((tq, 2 * d), BF16),
                        pltpu.VMEM((1, tq), F32),
                        pltpu.VMEM((FOX_VROWS, tq), F32)],
        compiler_params=pltpu.CompilerParams(
            dimension_semantics=("arbitrary", "arbitrary", "arbitrary"),
            vmem_limit_bytes=VMEM_LIMIT_V7X),
        name="fox_attn",
    )(proj, proj, proj, proj, f3, wq.reshape(1, d), wk.reshape(1, d), wo.reshape(1, d))


def _conv_silu(buf, src_ref, w_ref, bias, tile):
    buf[CONV_HALO:CONV_HALO + tile, :] = src_ref[...].astype(F32)
    base = CONV_HALO - (CONV_WIDTH - 1)
    acc = w_ref[0:1, :] * buf[base:base + tile, :]
    for k in range(1, CONV_WIDTH):
        acc = acc + w_ref[k:k + 1, :] * buf[base + k:base + k + tile, :]
    if bias is not None:
        acc = acc + bias
    buf[0:CONV_HALO, :] = buf[tile:tile + CONV_HALO, :]
    return _silu(acc)


def _ssd_kernel(z_ref, xs_ref, b_ref, c_ref, g_ref, cwx_ref, cwb_ref, cwc_ref,
                cbx_ref, cbb_ref, cbc_ref, dtb_ref, alog_ref, dexp_ref, nw_ref, e_ref,
                o_ref, bufx, bufb, bufc, ht_ref, *, tile, heads):
    L = CHUNK
    P = SSD_HEAD_DIM
    N = SSD_STATE
    per_group = heads // SSD_GROUPS
    gw = per_group * P

    @pl.when(pl.program_id(1) == 0)
    def _():
        bufx[0:CONV_HALO, :] = jnp.zeros((CONV_HALO, bufx.shape[1]), F32)
        bufb[0:CONV_HALO, :] = jnp.zeros((CONV_HALO, bufb.shape[1]), F32)
        bufc[0:CONV_HALO, :] = jnp.zeros((CONV_HALO, bufc.shape[1]), F32)
        ht_ref[...] = jnp.zeros(ht_ref.shape, F32)

    xs = _conv_silu(bufx, xs_ref, cwx_ref, cbx_ref[...], tile)
    bm = _conv_silu(bufb, b_ref, cwb_ref, cbb_ref[...], tile)
    cm = _conv_silu(bufc, c_ref, cwc_ref, cbc_ref[...], tile)

    dt_row = _softplus(g_ref[...] + dtb_ref[...])
    a_row = dt_row * (-jnp.exp(alog_ref[...]))

    r, c = _iota2((L, L))
    tril = r >= c
    tri_u = (r <= c).astype(BF16)
    tri_l = tril.astype(BF16)
    eye = (r == c).astype(BF16)
    expand = e_ref[...]

    for ci in range(tile // L):
        sl = slice(ci * L, (ci + 1) * L)
        a_c = a_row[:, sl]
        cs_row = _dot3(a_c, tri_u)
        cs_col = _dot3_nt(tri_l, a_c)
        dt_col = _dot3_nt(eye, dt_row[:, sl])
        total = cs_col[L - 1:L, :]
        ds_col = jnp.exp(total - cs_col)
        ecs_col = jnp.exp(cs_col)
        cd_row = jnp.broadcast_to(jnp.exp(total), (8, total.shape[1]))
        stacked = jnp.concatenate([dt_col, ds_col, ecs_col, cd_row], axis=0)
        s1 = stacked.astype(BF16)
        s2 = (stacked - s1.astype(F32)).astype(BF16)
        ex = _dot(s1, expand) + _dot(s2, expand)
        dt_exp = ex[0:L]
        ds_exp = ex[L:2 * L]
        ecs_exp = ex[2 * L:3 * L]
        cd_exp = ex[3 * L:3 * L + 1]

        xs_c = xs[sl]
        xdt = xs_c * dt_exp
        xw = (xdt * ds_exp).astype(BF16)
        xdt_b = xdt.astype(BF16)
        y_groups = []
        for g in range(SSD_GROUPS):
            bg = bm[sl, g * N:(g + 1) * N]
            cg = cm[sl, g * N:(g + 1) * N].astype(BF16)
            cb = _dot_nt(cg, bg.astype(BF16))
            ht = ht_ref[g]
            cols = slice(g * gw, (g + 1) * gw)
            y_off = _dot(cg, ht.astype(BF16)) * ecs_exp[:, cols]
            ys = []
            for rr in range(per_group):
                hd = g * per_group + rr
                seg = cs_col[:, hd:hd + 1] - cs_row[hd:hd + 1, :]
                lm = jnp.exp(jnp.where(tril, seg, NEG_BIG))
                ys.append(_dot((cb * lm).astype(BF16), xdt_b[:, hd * P:(hd + 1) * P]))
            y_groups.append(jnp.concatenate(ys, axis=1) + y_off)
            ht_ref[g] = ht * cd_exp[:, cols] + _dot_tn(bg.astype(BF16), xw[:, cols])
        y = jnp.concatenate(y_groups, axis=1)
        y = (y + xs_c * dexp_ref[...]) * _silu(z_ref[sl, :].astype(F32))
        outs = []
        for g in range(SSD_GROUPS):
            yg = y[:, g * gw:(g + 1) * gw]
            ms = jnp.mean(yg * yg, axis=-1, keepdims=True)
            outs.append(yg * lax.rsqrt(ms + EPS))
        o_ref[sl, :] = (jnp.concatenate(outs, axis=1) * nw_ref[...]).astype(o_ref.dtype)


def _ssd(proj, gt, conv_w, conv_b, dt_bias, a_log, d_skip, norm_w, bsz, seq, heads,
         z_blk, xs_blk, b_blk, c_blk):
    width = heads * SSD_HEAD_DIM
    bcw = SSD_GROUPS * SSD_STATE
    tile = _pick_tile(seq, (256, 128, 64))
    nt = seq // tile
    rows = SSD_GATE_ROWS
    pad = rows - heads

    def col(v):
        return jnp.pad(v.astype(F32), (0, pad)).reshape(rows, 1)

    head_of_col = jnp.arange(width) // SSD_HEAD_DIM
    expand = (jnp.arange(rows)[:, None] == head_of_col[None, :]).astype(BF16)
    d_exp = jnp.repeat(d_skip.astype(F32), SSD_HEAD_DIM).reshape(1, width)
    cwx, cwb, cwc = conv_w[:, :width], conv_w[:, width:width + bcw], conv_w[:, width + bcw:]
    cbx = conv_b[:width].reshape(1, width)
    cbb = conv_b[width:width + bcw].reshape(1, bcw)
    cbc = conv_b[width + bcw:].reshape(1, bcw)

    def full(shape):
        return pl.BlockSpec(shape, lambda b, t: (0,) * len(shape))

    kern = functools.partial(_ssd_kernel, tile=tile, heads=heads)
    return pl.pallas_call(
        kern,
        out_shape=jax.ShapeDtypeStruct((bsz * seq, width), BF16),
        grid=(bsz, nt),
        in_specs=[pl.BlockSpec((tile, width), lambda b, t: (b * nt + t, z_blk)),
                  pl.BlockSpec((tile, width), lambda b, t: (b * nt + t, xs_blk)),
                  pl.BlockSpec((tile, bcw), lambda b, t: (b * nt + t, b_blk)),
                  pl.BlockSpec((tile, bcw), lambda b, t: (b * nt + t, c_blk)),
                  pl.BlockSpec((rows, tile), lambda b, t: (0, b * nt + t)),
                  full((CONV_WIDTH, width)), full((CONV_WIDTH, bcw)), full((CONV_WIDTH, bcw)),
                  full((1, width)), full((1, bcw)), full((1, bcw)),
                  full((rows, 1)), full((rows, 1)), full((1, width)), full((1, width)),
                  full((rows, width))],
        out_specs=pl.BlockSpec((tile, width), lambda b, t: (b * nt + t, 0)),
        scratch_shapes=[pltpu.VMEM((tile + CONV_HALO, width), F32),
                        pltpu.VMEM((tile + CONV_HALO, bcw), F32),
                        pltpu.VMEM((tile + CONV_HALO, bcw), F32),
                        pltpu.VMEM((SSD_GROUPS, SSD_STATE, width // SSD_GROUPS), F32)],
        compiler_params=pltpu.CompilerParams(dimension_semantics=("arbitrary", "arbitrary"),
                                             vmem_limit_bytes=VMEM_LIMIT_V7X),
        name="ssd",
    )(proj, proj, proj, proj, gt, cwx, cwb, cwc, cbx, cbb, cbc, col(dt_bias), col(a_log),
      d_exp, norm_w.reshape(1, width), expand)


def _unit_lower_inverses(n_list, r, c):
    eye = (r == c).astype(F32)
    same2 = (r >> 1) == (c >> 1)
    ts = [eye - jnp.where(same2, n, 0.0) for n in n_list]
    s = 2
    while s < CHUNK:
        sh = s.bit_length()
        in_block = (r >> sh) == (c >> sh)
        low_left = in_block & ((r & (2 * s - 1)) >= s) & ((c & (2 * s - 1)) < s)
        tbs = [t.astype(BF16) for t in ts]
        xs = [_dot(jnp.where(low_left, n, 0.0).astype(BF16), tb) for n, tb in zip(n_list, tbs)]
        ts = [t - _dot(tb, x.astype(BF16)) for t, tb, x in zip(ts, tbs, xs)]
        s *= 2
    return ts


def _gdn_kernel(q_ref, k_ref, v_ref, z_ref, g_ref, cwq_ref, cwk_ref, cwv_ref, dtb_ref, alog_ref,
                nw_ref, o_ref, bufq, bufk, bufv, s_ref, *, tile, hg, scale):
    L = CHUNK
    D = HEAD_DIM

    @pl.when(pl.program_id(2) == 0)
    def _():
        bufq[0:CONV_HALO, :] = jnp.zeros((CONV_HALO, bufq.shape[1]), F32)
        bufk[0:CONV_HALO, :] = jnp.zeros((CONV_HALO, bufk.shape[1]), F32)
        bufv[0:CONV_HALO, :] = jnp.zeros((CONV_HALO, bufv.shape[1]), F32)
        s_ref[...] = jnp.zeros(s_ref.shape, F32)

    q_all = _conv_silu(bufq, q_ref, cwq_ref, None, tile)
    k_all = _conv_silu(bufk, k_ref, cwk_ref, None, tile)
    v_all = _conv_silu(bufv, v_ref, cwv_ref, None, tile)

    gl = g_ref[...]
    rows8 = lax.broadcasted_iota(jnp.int32, gl.shape, 0)
    decay_log = -jnp.exp(alog_ref[0]) * _softplus(gl + dtb_ref[0])
    gates = jnp.where(rows8 < 4, decay_log, jax.nn.sigmoid(gl))
    gates = jnp.concatenate([gates, jnp.zeros_like(gates)], axis=0)

    r, c = _iota2((L, L))
    incl = r >= c
    strict = r > c
    tri_u = (r <= c).astype(BF16)
    tri_eye = jnp.concatenate([incl.astype(BF16), (r == c).astype(BF16)], axis=0)

    nchunk = tile // L
    cols_c = [_dot3_nt(tri_eye, gates[:, ci * L:(ci + 1) * L]) for ci in range(nchunk)]
    rows_c = [_dot3(gates[:, ci * L:(ci + 1) * L], tri_u) for ci in range(nchunk)]

    qn, kn, vh = [], [], []
    for h in range(hg):
        hs = slice(h * D, (h + 1) * D)
        qh = q_all[:, hs]
        kh = k_all[:, hs]
        qn.append(qh * (lax.rsqrt(jnp.sum(qh * qh, axis=-1, keepdims=True) + EPS) * scale))
        kn.append(kh * lax.rsqrt(jnp.sum(kh * kh, axis=-1, keepdims=True) + EPS))
        vh.append(v_all[:, hs])

    units = [(h, ci) for ci in range(nchunk) for h in range(hg)]
    loc = {}
    n_list = []
    for (h, ci) in units:
        sl = slice(ci * L, (ci + 1) * L)
        gcc = cols_c[ci][0:L, h:h + 1]
        gcr = rows_c[ci][h:h + 1, :]
        beta = cols_c[ci][L:2 * L, 4 + h:5 + h]
        g_last = gcc[L - 1:L, :]
        decay = jnp.exp(jnp.where(incl, gcc - gcr, NEG_BIG))
        kc = kn[h][sl]
        qc = qn[h][sl]
        kc_b = kc.astype(BF16)
        k_beta = kc * beta
        kk = _dot_nt(k_beta.astype(BF16), kc_b) * decay
        n_list.append(jnp.where(strict, kk, 0.0))
        rhs = jnp.concatenate([vh[h][sl] * beta, k_beta * jnp.exp(gcc)], axis=1)
        loc[(h, ci)] = dict(
            rhs=rhs.astype(BF16),
            qk=(_dot_nt(qc.astype(BF16), kc_b) * decay).astype(BF16),
            k_tail=(kc * jnp.exp(g_last - gcc)).astype(BF16),
            q_dec=(qc * jnp.exp(gcc)).astype(BF16),
            e_last=jnp.exp(g_last))
    t_invs = _unit_lower_inverses(n_list, r, c)
    for unit, t_inv in zip(units, t_invs):
        loc[unit]["sol"] = _dot(t_inv.astype(BF16), loc[unit]["rhs"])

    states = [s_ref[h] for h in range(hg)]
    outs = {}
    for ci in range(nchunk):
        sb = [st.astype(BF16) for st in states]
        v_new = []
        for h in range(hg):
            sol = loc[(h, ci)]["sol"]
            v_new.append((sol[:, 0:D] - _dot(sol[:, D:2 * D].astype(BF16), sb[h])).astype(BF16))
        for h in range(hg):
            u = loc[(h, ci)]
            outs[(h, ci)] = _dot(u["q_dec"], sb[h]) + _dot(u["qk"], v_new[h])
            states[h] = states[h] * u["e_last"] + _dot_tn(u["k_tail"], v_new[h])
    for h in range(hg):
        s_ref[h] = states[h]

    rows = []
    for ci in range(nchunk):
        heads_out = []
        for h in range(hg):
            o = outs[(h, ci)]
            ms = jnp.mean(o * o, axis=-1, keepdims=True)
            heads_out.append(o * lax.rsqrt(ms + EPS) * nw_ref[...])
        rows.append(jnp.concatenate(heads_out, axis=1))
    o_all = jnp.concatenate(rows, axis=0) if nchunk > 1 else rows[0]
    o_ref[...] = (o_all * _silu(z_ref[...].astype(F32))).astype(o_ref.dtype)


def _gdn(proj, gt, conv_w, dt_bias, a_log, norm_w, bsz, seq, heads, hg, q_blk, k_blk, v_blk,
         z_blk, gate_blk):
    d = HEAD_DIM
    width = heads * d
    gwid = hg * d
    ngrp = heads // hg
    tile = _pick_tile(seq, (128, 64))
    nt = seq // tile
    cwq, cwk, cwv = conv_w[:, :width], conv_w[:, width:2 * width], conv_w[:, 2 * width:]

    def grp_col(v):
        v = jnp.pad(v.astype(F32).reshape(ngrp, hg), ((0, 0), (0, GDN_GATE_ROWS - hg)))
        return v.reshape(ngrp, GDN_GATE_ROWS, 1)

    def tok(blk):
        return pl.BlockSpec((tile, gwid), lambda b, g, t: (b * nt + t, blk + g))

    def cw():
        return pl.BlockSpec((CONV_WIDTH, gwid), lambda b, g, t: (0, g))

    def gcol():
        return pl.BlockSpec((1, GDN_GATE_ROWS, 1), lambda b, g, t: (g, 0, 0))

    kern = functools.partial(_gdn_kernel, tile=tile, hg=hg, scale=d ** -0.5)
    return pl.pallas_call(
        kern,
        out_shape=jax.ShapeDtypeStruct((bsz * seq, width), BF16),
        grid=(bsz, ngrp, nt),
        in_specs=[tok(q_blk), tok(k_blk), tok(v_blk), tok(z_blk),
                  pl.BlockSpec((GDN_GATE_ROWS, tile), lambda b, g, t: (gate_blk + g, b * nt + t)),
                  cw(), cw(), cw(), gcol(), gcol(),
                  pl.BlockSpec((1, d), lambda b, g, t: (0, 0))],
        out_specs=pl.BlockSpec((tile, gwid), lambda b, g, t: (b * nt + t, g)),
        scratch_shapes=[pltpu.VMEM((tile + CONV_HALO, gwid), F32),
                        pltpu.VMEM((tile + CONV_HALO, gwid), F32),
                        pltpu.VMEM((tile + CONV_HALO, gwid), F32),
                        pltpu.VMEM((hg, d, d), F32)],
        compiler_params=pltpu.CompilerParams(
            dimension_semantics=("arbitrary", "arbitrary", "arbitrary"),
            vmem_limit_bytes=VMEM_LIMIT_V7X),
        name="gdn",
    )(proj, proj, proj, proj, gt, cwq, cwk, cwv, grp_col(dt_bias), grp_col(a_log),
      norm_w.reshape(1, d))


def _out_proj_kernel(x_ref, mf_ref, ms_ref, mg_ref, wf_ref, ws_ref, wg_ref, o_ref):
    acc = _dot(mf_ref[...], wf_ref[...])
    acc = acc + _dot(ms_ref[...], ws_ref[...])
    acc = acc + _dot(mg_ref[...], wg_ref[...])
    o_ref[...] = x_ref[...] + acc


def _out_proj(x2d, mf, ms, mg, wf, ws, wg):
    m, d = x2d.shape
    tm = _pick_tile(m, (1024, 512, 256, 128))
    tn = _pick_tile(d, (1024, 512, 256, 128))

    def act(a):
        return pl.BlockSpec((tm, a.shape[1]), lambda i, j: (i, 0))

    def wgt(a):
        return pl.BlockSpec((a.shape[0], tn), lambda i, j: (0, j))

    return pl.pallas_call(
        _out_proj_kernel,
        out_shape=jax.ShapeDtypeStruct((m, d), F32),
        grid=(m // tm, d // tn),
        in_specs=[pl.BlockSpec((tm, tn), lambda i, j: (i, j)),
                  act(mf), act(ms), act(mg), wgt(wf), wgt(ws), wgt(wg)],
        out_specs=pl.BlockSpec((tm, tn), lambda i, j: (i, j)),
        compiler_params=pltpu.CompilerParams(dimension_semantics=("arbitrary", "arbitrary"),
                                             vmem_limit_bytes=VMEM_LIMIT_V7X),
        name="out_proj",
    )(x2d, mf, ms, mg, wf, ws, wg)


def _layer(x2d, bsz, seq, norm_w, w_in, w_out, fox_b_f, fox_q_norm_w, fox_k_norm_w,
           fox_out_norm_w, ssd_conv_w, ssd_conv_b, ssd_dt_bias, ssd_a_log, ssd_d, ssd_norm_w,
           gdn_conv_w, gdn_dt_bias, gdn_a_log, gdn_norm_w):
    d_model = x2d.shape[1]
    fox_w = d_model // 4
    ssd_w = 3 * d_model // 8
    gdn_w = d_model - fox_w - ssd_w
    fox_h = fox_w // HEAD_DIM
    ssd_h = ssd_w // SSD_HEAD_DIM
    gdn_h = gdn_w // HEAD_DIM
    bcw = SSD_GROUPS * SSD_STATE
    hg = 4 if gdn_h % 4 == 0 else 2
    ngrp = gdn_h // hg
    assert fox_h <= FOX_GATE_ROWS and ssd_h <= SSD_GATE_ROWS and ssd_h % SSD_GROUPS == 0
    assert gdn_h % hg == 0 and seq % CHUNK == 0

    sizes = (3 * fox_w, fox_h, fox_w, ssd_w + 2 * bcw, ssd_w, ssd_h, 3 * gdn_w, gdn_w, gdn_h, gdn_h)
    offs = [0]
    for s in sizes:
        offs.append(offs[-1] + s)
    (o_fqkv, o_ff, o_fz, o_sxbc, o_sz, o_sdt, o_gqkv, o_gz, o_gb, o_ga) = offs[:-1]

    def cols(o, n):
        return w_in[:, o:o + n]

    w_main = jnp.concatenate(
        [cols(o_sz, ssd_w), cols(o_sxbc, ssd_w + 2 * bcw), cols(o_fqkv, 3 * fox_w),
         cols(o_fz, fox_w), cols(o_gqkv, 3 * gdn_w), cols(o_gz, gdn_w)], axis=1).astype(BF16)
    c_sz, c_sxs, c_sb = 0, ssd_w, 2 * ssd_w
    c_sc = c_sb + bcw
    c_fq = c_sc + bcw
    c_fz = c_fq + 3 * fox_w
    c_gq = c_fz + fox_w
    c_gz = c_gq + 3 * gdn_w
    gwid = hg * HEAD_DIM
    assert c_sb % bcw == 0 and c_fq % HEAD_DIM == 0 and c_gq % gwid == 0 and gdn_w % gwid == 0

    def rows(o, n, total):
        return jnp.pad(w_in[:, o:o + n].T, ((0, total - n), (0, 0)))

    gate_rows = [rows(o_sdt, ssd_h, SSD_GATE_ROWS), rows(o_ff, fox_h, FOX_GATE_ROWS)]
    for g in range(ngrp):
        gate_rows.append(rows(o_ga + g * hg, hg, 4))
        gate_rows.append(rows(o_gb + g * hg, hg, 4))
    wgt = jnp.concatenate(gate_rows, axis=0).astype(BF16)
    fox_row_blk = SSD_GATE_ROWS // FOX_GATE_ROWS
    gdn_row_blk = (SSD_GATE_ROWS + FOX_GATE_ROWS) // GDN_GATE_ROWS

    h, gt = _prenorm(x2d, norm_w, wgt)
    proj = _in_proj(h, w_main)

    b_f = jnp.pad(fox_b_f.astype(F32), (0, FOX_GATE_ROWS - fox_h)).reshape(FOX_GATE_ROWS, 1)
    f_cum = _fox_gate(gt, b_f, bsz, seq, fox_row_blk)
    f3 = f_cum.reshape(bsz * FOX_GATE_ROWS, 1, seq)
    mix_fox = _fox_attn(proj, f3, fox_q_norm_w, fox_k_norm_w, fox_out_norm_w, bsz, seq, fox_h,
                        c_fq // HEAD_DIM, (c_fq + fox_w) // HEAD_DIM,
                        (c_fq + 2 * fox_w) // HEAD_DIM, c_fz // HEAD_DIM)
    mix_ssd = _ssd(proj, gt, ssd_conv_w, ssd_conv_b, ssd_dt_bias, ssd_a_log, ssd_d, ssd_norm_w,
                   bsz, seq, ssd_h, c_sz // ssd_w, c_sxs // ssd_w, c_sb // bcw, c_sc // bcw)
    mix_gdn = _gdn(proj, gt, gdn_conv_w, gdn_dt_bias, gdn_a_log, gdn_norm_w, bsz, seq, gdn_h, hg,
                   c_gq // gwid, (c_gq + gdn_w) // gwid, (c_gq + 2 * gdn_w) // gwid,
                   c_gz // gwid, gdn_row_blk)

    w_o = w_out.astype(BF16)
    return _out_proj(x2d, mix_fox, mix_ssd, mix_gdn, w_o[:fox_w], w_o[fox_w:fox_w + ssd_w],
                     w_o[fox_w + ssd_w:])


def kernel(x, norm_w, w_in, w_out, fox_b_f, fox_q_norm_w, fox_k_norm_w, fox_out_norm_w, ssd_conv_w, ssd_conv_b, ssd_dt_bias, ssd_A_log, ssd_D, ssd_norm_w, gdn_conv_w, gdn_dt_bias, gdn_A_log, gdn_norm_w):
    bsz, seq, d_model = x.shape
    x2d = x.reshape(bsz * seq, d_model)
    for l in range(norm_w.shape[0]):
        x2d = _layer(x2d, bsz, seq, norm_w[l], w_in[l], w_out[l], fox_b_f[l], fox_q_norm_w[l],
                     fox_k_norm_w[l], fox_out_norm_w[l], ssd_conv_w[l], ssd_conv_b[l],
                     ssd_dt_bias[l], ssd_A_log[l], ssd_D[l], ssd_norm_w[l], gdn_conv_w[l],
                     gdn_dt_bias[l], gdn_A_log[l], gdn_norm_w[l])
    return x2d.reshape(bsz, seq, d_model)
```

```python
import functools

import jax
import jax.numpy as jnp
from jax import lax
from jax.experimental import pallas as pl
from jax.experimental.pallas import tpu as pltpu

F32 = jnp.float32
BF16 = jnp.bfloat16

EPS = 1e-6
CHUNK = 64
HEAD_DIM = 128
SSD_HEAD_DIM = 64
SSD_GROUPS = 4
SSD_STATE = 128
CONV_WIDTH = 4
CONV_HALO = 16
SSD_GATE_ROWS = 32
FOX_GATE_ROWS = 8
GDN_GATE_ROWS = 8
NEG_BIG = -1e30
VMEM_LIMIT_V7X = 56 * 1024 * 1024


def _dot(a, b):
    return jnp.dot(a, b, preferred_element_type=F32)


def _dot_nt(a, b):
    return lax.dot_general(a, b, (((1,), (1,)), ((), ())), preferred_element_type=F32)


def _dot_tn(a, b):
    return lax.dot_general(a, b, (((0,), (0,)), ((), ())), preferred_element_type=F32)


def _split3(a):
    a1 = a.astype(BF16)
    r1 = a - a1.astype(F32)
    a2 = r1.astype(BF16)
    r2 = r1 - a2.astype(F32)
    return a1, a2, r2.astype(BF16)


def _dot3(a, m):
    p1, p2, p3 = _split3(a)
    return _dot(p1, m) + _dot(p2, m) + _dot(p3, m)


def _dot3_nt(m, a):
    p1, p2, p3 = _split3(a)
    return _dot_nt(m, p1) + _dot_nt(m, p2) + _dot_nt(m, p3)


def _silu(x):
    return x * jax.nn.sigmoid(x)


def _softplus(x):
    return jnp.maximum(x, 0.0) + jnp.log1p(jnp.exp(-jnp.abs(x)))


def _log_sigmoid(x):
    return jnp.minimum(x, 0.0) - jnp.log1p(jnp.exp(-jnp.abs(x)))


def _iota2(shape):
    return (lax.broadcasted_iota(jnp.int32, shape, 0), lax.broadcasted_iota(jnp.int32, shape, 1))


def _pick_tile(n, candidates):
    for c in candidates:
        if n % c == 0 and n // c >= 2:
            return c
    return n


def _prenorm_kernel(x_ref, nw_ref, wgt_ref, h_ref, gt_ref):
    x = x_ref[...]
    ms = jnp.mean(x * x, axis=-1, keepdims=True)
    hb = (x * lax.rsqrt(ms + EPS) * nw_ref[...]).astype(BF16)
    h_ref[...] = hb
    gt_ref[...] = _dot_nt(wgt_ref[...], hb)


def _prenorm(x2d, norm_w, wgt):
    m, d = x2d.shape
    gr = wgt.shape[0]
    tm = _pick_tile(m, (512, 256, 128))
    return pl.pallas_call(
        _prenorm_kernel,
        out_shape=(jax.ShapeDtypeStruct((m, d), BF16), jax.ShapeDtypeStruct((gr, m), F32)),
        grid=(m // tm,),
        in_specs=[pl.BlockSpec((tm, d), lambda i: (i, 0)),
                  pl.BlockSpec((1, d), lambda i: (0, 0)),
                  pl.BlockSpec((gr, d), lambda i: (0, 0))],
        out_specs=(pl.BlockSpec((tm, d), lambda i: (i, 0)),
                   pl.BlockSpec((gr, tm), lambda i: (0, i))),
        compiler_params=pltpu.CompilerParams(dimension_semantics=("arbitrary",),
                                             vmem_limit_bytes=VMEM_LIMIT_V7X),
        name="prenorm_gates",
    )(x2d, norm_w.reshape(1, d), wgt)


def _in_proj_kernel(h_ref, w_ref, o_ref):
    o_ref[...] = _dot(h_ref[...], w_ref[...]).astype(o_ref.dtype)


def _in_proj(h, w):
    m, d = h.shape
    n = w.shape[1]
    tm = _pick_tile(m, (1024, 512, 256, 128))
    tn = _pick_tile(n, (1024, 512, 256, 128))
    return pl.pallas_call(
        _in_proj_kernel,
        out_shape=jax.ShapeDtypeStruct((m, n), BF16),
        grid=(m // tm, n // tn),
        in_specs=[pl.BlockSpec((tm, d), lambda i, j: (i, 0)),
                  pl.BlockSpec((d, tn), lambda i, j: (0, j))],
        out_specs=pl.BlockSpec((tm, tn), lambda i, j: (i, j)),
        compiler_params=pltpu.CompilerParams(dimension_semantics=("arbitrary", "arbitrary"),
                                             vmem_limit_bytes=VMEM_LIMIT_V7X),
        name="in_proj",
    )(h, w)


def _fox_gate_kernel(g_ref, b_ref, f_ref):
    seq = g_ref.shape[1]
    lf = _log_sigmoid(g_ref[...] + b_ref[...])
    r, c = _iota2((128, 128))
    upper = (r <= c).astype(BF16)
    carry = jnp.zeros((lf.shape[0], 1), F32)
    for i in range(seq // 128):
        cs = _dot3(lf[:, i * 128:(i + 1) * 128], upper) + carry
        f_ref[0, :, i * 128:(i + 1) * 128] = cs
        carry = cs[:, 127:128]


def _fox_gate(gt, b_f, bsz, seq, row_block):
    rows = FOX_GATE_ROWS
    return pl.pallas_call(
        _fox_gate_kernel,
        out_shape=jax.ShapeDtypeStruct((bsz, rows, seq), F32),
        grid=(bsz,),
        in_specs=[pl.BlockSpec((rows, seq), lambda b: (row_block, b)),
                  pl.BlockSpec((rows, 1), lambda b: (0, 0))],
        out_specs=pl.BlockSpec((1, rows, seq), lambda b: (b, 0, 0)),
        compiler_params=pltpu.CompilerParams(dimension_semantics=("arbitrary",)),
        name="fox_gate",
    )(gt, b_f)


LOG2E = 1.4426950408889634
FOX_SPLIT = 2
FOX_VROWS = 144


def _fox_attn_kernel(q_ref, k_ref, v_ref, z_ref, f_ref, wq_ref, wk_ref, wo_ref, o_ref,
                     ka_ref, vat_ref, qa_ref, m_ref, acc_ref, s_ref, *, tq, scale):
    qi = pl.program_id(2)
    seq = k_ref.shape[0]
    d = HEAD_DIM
    tw = tq // FOX_SPLIT
    lane = lax.broadcasted_iota(jnp.int32, (tq, d), 1)

    @pl.when(qi == 0)
    def _():
        r, c = _iota2((tq, tq))
        eye = (r == c).astype(BF16)
        row8 = lax.broadcasted_iota(jnp.int32, (8, tq), 0)
        ones_row = (lax.broadcasted_iota(jnp.int32, (FOX_VROWS - d, tq), 0) == 0).astype(BF16)

        def body(i, carry):
            r0 = pl.multiple_of(i * tq, tq)
            kk = k_ref[pl.ds(r0, tq), :].astype(F32)
            ms = jnp.mean(kk * kk, axis=-1, keepdims=True)
            ka_ref[pl.ds(r0, tq), 0:d] = (kk * lax.rsqrt(ms + EPS) * wk_ref[...]).astype(BF16)
            frow = f_ref[0, :, pl.ds(r0, tq)]
            b1, b2, b3 = _split3((frow[:, 0:1] - frow) * LOG2E)
            parts = jnp.where(row8 == 0, b1.astype(F32),
                              jnp.where(row8 == 1, b2.astype(F32),
                                        jnp.where(row8 == 2, b3.astype(F32), 0.0)))
            parts = jnp.concatenate([parts, jnp.zeros((d - 8, tq), F32)], axis=0).astype(BF16)
            ka_ref[pl.ds(r0, tq), d:2 * d] = _dot_nt(eye, parts).astype(BF16)
            vt = jnp.transpose(v_ref[pl.ds(r0, tq), :].astype(F32))
            vat_ref[i, 0:d, :] = vt.astype(BF16)
            vat_ref[i, d:FOX_VROWS, :] = ones_row
            return carry
        lax.fori_loop(0, seq // tq, body, 0)

    q = q_ref[...].astype(F32)
    ms = jnp.mean(q * q, axis=-1, keepdims=True)
    qa_ref[:, 0:d] = (q * lax.rsqrt(ms + EPS) * (wq_ref[...] * (scale * LOG2E))).astype(BF16)
    qa_ref[:, d:2 * d] = (lane < 3).astype(BF16)
    q0 = pl.multiple_of(qi * tq, tq)
    f_first = f_ref[0, :, pl.ds(q0, tq)][:, 0:1]

    m_ref[...] = jnp.full(m_ref.shape, NEG_BIG, F32)
    acc_ref[...] = jnp.zeros(acc_ref.shape, F32)

    def scores_to(slot, kb):
        r0 = pl.multiple_of(kb * tq, tq)
        s_ref[slot] = _dot_nt(ka_ref[pl.ds(r0, tq), :], qa_ref[...])

    def consume(kb, slot, masked):
        r0 = pl.multiple_of(kb * tq, tq)
        shift = (f_first - f_ref[0, :, pl.ds(r0, tq)][:, 0:1]) * LOG2E
        new_m, new_acc = [], []
        for h in range(FOX_SPLIT):
            cols = slice(h * tw, (h + 1) * tw)
            nk = (h + 1) * tw if masked else tq
            s = s_ref[slot, 0:nk, cols]
            if masked:
                r, c = _iota2((nk, tw))
                s = jnp.where(r <= c + h * tw, s, NEG_BIG)
            m_prev = m_ref[:, cols]
            m_new = jnp.maximum(m_prev, jnp.max(s, axis=0, keepdims=True) + shift)
            p = jnp.exp2(s - (m_new - shift)).astype(BF16)
            alpha = jnp.exp2(m_prev - m_new)
            pv = _dot(vat_ref[kb, :, 0:nk], p)
            new_acc.append(alpha * acc_ref[:, cols] + pv)
            new_m.append(m_new)
        m_ref[...] = jnp.concatenate(new_m, axis=1)
        acc_ref[...] = jnp.concatenate(new_acc, axis=1)

    scores_to(0, 0)

    def pair_body(j, carry):
        kb = 2 * j
        scores_to(1, kb + 1)
        consume(kb, 0, False)
        scores_to(0, kb + 2)
        consume(kb + 1, 1, False)
        return carry
    lax.fori_loop(0, qi // 2, pair_body, 0)

    @pl.when(qi % 2 == 1)
    def _():
        scores_to(1, qi)
        consume(qi - 1, 0, False)
        consume(qi, 1, True)

    @pl.when(qi % 2 == 0)
    def _():
        consume(qi, 0, True)

    o = jnp.transpose(acc_ref[0:d, :] / acc_ref[d:d + 1, :])
    ms = jnp.mean(o * o, axis=-1, keepdims=True)
    o = o * lax.rsqrt(ms + EPS) * wo_ref[...]
    o_ref[...] = (o * _silu(z_ref[...].astype(F32))).astype(o_ref.dtype)


def _fox_attn(proj, f3, wq, wk, wo, bsz, seq, heads, q_blk, k_blk, v_blk, z_blk):
    d = HEAD_DIM
    tq = _pick_tile(seq, (512, 256, 128))
    nq = seq // tq
    kern = functools.partial(_fox_attn_kernel, tq=tq, scale=d ** -0.5)
    vec = pl.BlockSpec((1, d), lambda b, h, i: (0, 0))
    return pl.pallas_call(
        kern,
        out_shape=jax.ShapeDtypeStruct((bsz * seq, heads * d), BF16),
        grid=(bsz, heads, nq),
        in_specs=[pl.BlockSpec((tq, d), lambda b, h, i: (b * nq + i, q_blk + h)),
                  pl.BlockSpec((seq, d), lambda b, h, i: (b, k_blk + h)),
                  pl.BlockSpec((seq, d), lambda b, h, i: (b, v_blk + h)),
                  pl.BlockSpec((tq, d), lambda b, h, i: (b * nq + i, z_blk + h)),
                  pl.BlockSpec((1, 1, seq), lambda b, h, i: (b * FOX_GATE_ROWS + h, 0, 0)),
                  vec, vec, vec],
        out_specs=pl.BlockSpec((tq, d), lambda b, h, i: (b * nq + i, h)),
        scratch_shapes=[pltpu.VMEM((seq, 2 * d), BF16),
                        pltpu.VMEM((nq, FOX_VROWS, tq), BF16),
                        pltpu.VMEM((tq, 2 * d), BF16),
                        pltpu.VMEM((1, tq), F32),
                        pltpu.VMEM((FOX_VROWS, tq), F32),
                        pltpu.VMEM((2, tq, tq), F32)],
        compiler_params=pltpu.CompilerParams(
            dimension_semantics=("arbitrary", "arbitrary", "arbitrary"),
            vmem_limit_bytes=VMEM_LIMIT_V7X),
        name="fox_attn",
    )(proj, proj, proj, proj, f3, wq.reshape(1, d), wk.reshape(1, d), wo.reshape(1, d))


def _conv_silu(buf, ubuf, src_ref, w_ref, bias, tile):
    assert CONV_WIDTH == 4 and CONV_HALO >= 16
    h0 = CONV_HALO
    width = buf.shape[1]
    buf[h0:h0 + tile, :] = src_ref[...].astype(F32)
    x = buf[h0 - 8:h0 + tile, :]
    z = buf[h0 - 9:h0 + tile - 1, :]
    ubuf[0:tile + 8, 0:width] = w_ref[1:2, :] * x + w_ref[0:1, :] * z
    acc = w_ref[3:4, :] * x[8:] + w_ref[2:3, :] * z[8:] + ubuf[6:tile + 6, 0:width]
    if bias is not None:
        acc = acc + bias
    buf[0:h0, :] = buf[tile:tile + h0, :]
    return _silu(acc)


def _ssd_kernel(z_ref, xs_ref, b_ref, c_ref, g_ref, cwx_ref, cwb_ref, cwc_ref,
                cbx_ref, cbb_ref, cbc_ref, dtb_ref, alog_ref, dexp_ref, nw_ref, e_ref,
                o_ref, bufx, bufb, bufc, ubuf, ht_ref, *, tile, heads):
    L = CHUNK
    P = SSD_HEAD_DIM
    N = SSD_STATE
    per_group = heads // SSD_GROUPS
    gw = per_group * P
    nchunk = tile // L

    @pl.when(pl.program_id(1) == 0)
    def _():
        bufx[0:CONV_HALO, :] = jnp.zeros((CONV_HALO, bufx.shape[1]), F32)
        bufb[0:CONV_HALO, :] = jnp.zeros((CONV_HALO, bufb.shape[1]), F32)
        bufc[0:CONV_HALO, :] = jnp.zeros((CONV_HALO, bufc.shape[1]), F32)
        ht_ref[...] = jnp.zeros(ht_ref.shape, F32)

    xs = _conv_silu(bufx, ubuf, xs_ref, cwx_ref, cbx_ref[...], tile)
    bm = _conv_silu(bufb, ubuf, b_ref, cwb_ref, cbb_ref[...], tile)
    cm = _conv_silu(bufc, ubuf, c_ref, cwc_ref, cbc_ref[...], tile)
    bm_b = bm.astype(BF16)
    cm_b = cm.astype(BF16)

    dt_row = _softplus(g_ref[...] + dtb_ref[...])
    a_row = dt_row * (-jnp.exp(alog_ref[...]))

    r, c = _iota2((L, L))
    tril = r >= c
    tri_u = (r <= c).astype(BF16)
    tri_l = tril.astype(BF16)
    eye = (r == c).astype(BF16)
    expand = e_ref[...]
    first_half = lax.broadcasted_iota(jnp.int32, (L, 2 * P), 1) < P

    chunks = []
    for ci in range(nchunk):
        sl = slice(ci * L, (ci + 1) * L)
        a_c = a_row[:, sl]
        cs_row = _dot3(a_c, tri_u)
        cs_col = _dot3_nt(tri_l, a_c)
        dt_col = _dot3_nt(eye, dt_row[:, sl])
        total = cs_col[L - 1:L, :]
        ds_col = jnp.exp(total - cs_col)
        ecs_col = jnp.exp(cs_col)
        cd_row = jnp.broadcast_to(jnp.exp(total), (8, total.shape[1]))
        stacked = jnp.concatenate([dt_col, ds_col, ecs_col, cd_row], axis=0)
        s1 = stacked.astype(BF16)
        s2 = (stacked - s1.astype(F32)).astype(BF16)
        ex = _dot(s1, expand) + _dot(s2, expand)
        xs_c = xs[sl]
        xdt = xs_c * ex[0:L]
        xdt_b = xdt.astype(BF16)
        cbs = [_dot_nt(cm_b[sl, g * N:(g + 1) * N], bm_b[sl, g * N:(g + 1) * N])
               for g in range(SSD_GROUPS)]
        y_pairs = []
        for hp in range(heads // 2):
            xpair = xdt_b[:, hp * 2 * P:(hp + 1) * 2 * P]
            acc = None
            for hd, keep in ((2 * hp, first_half), (2 * hp + 1, ~first_half)):
                seg = cs_col[:, hd:hd + 1] - cs_row[hd:hd + 1, :]
                lm = jnp.exp(jnp.where(tril, seg, NEG_BIG))
                part = _dot((cbs[hd // per_group] * lm).astype(BF16),
                            jnp.where(keep, xpair, jnp.zeros_like(xpair)))
                acc = part if acc is None else acc + part
            y_pairs.append(acc)
        chunks.append(dict(
            y_diag=jnp.concatenate(y_pairs, axis=1),
            xw=(xdt * ex[L:2 * L]).astype(BF16),
            ecs=ex[2 * L:3 * L], cd=ex[3 * L:3 * L + 1], xs=xs_c))

    hts = [ht_ref[g] for g in range(SSD_GROUPS)]
    rows_out = []
    for ci in range(nchunk):
        sl = slice(ci * L, (ci + 1) * L)
        ck = chunks[ci]
        y_off = []
        for g in range(SSD_GROUPS):
            cols = slice(g * gw, (g + 1) * gw)
            y_off.append(_dot(cm_b[sl, g * N:(g + 1) * N], hts[g].astype(BF16)) * ck["ecs"][:, cols])
            hts[g] = hts[g] * ck["cd"][:, cols] + _dot_tn(bm_b[sl, g * N:(g + 1) * N], ck["xw"][:, cols])
        y = ck["y_diag"] + jnp.concatenate(y_off, axis=1)
        y = (y + ck["xs"] * dexp_ref[...]) * _silu(z_ref[sl, :].astype(F32))
        outs = []
        for g in range(SSD_GROUPS):
            yg = y[:, g * gw:(g + 1) * gw]
            ms = jnp.mean(yg * yg, axis=-1, keepdims=True)
            outs.append(yg * lax.rsqrt(ms + EPS))
        rows_out.append(jnp.concatenate(outs, axis=1) * nw_ref[...])
    for g in range(SSD_GROUPS):
        ht_ref[g] = hts[g]
    o_all = jnp.concatenate(rows_out, axis=0) if nchunk > 1 else rows_out[0]
    o_ref[...] = o_all.astype(o_ref.dtype)


def _ssd(proj, gt, conv_w, conv_b, dt_bias, a_log, d_skip, norm_w, bsz, seq, heads,
         z_blk, xs_blk, b_blk, c_blk):
    width = heads * SSD_HEAD_DIM
    bcw = SSD_GROUPS * SSD_STATE
    tile = _pick_tile(seq, (256, 128, 64))
    nt = seq // tile
    rows = SSD_GATE_ROWS
    pad = rows - heads

    def col(v):
        return jnp.pad(v.astype(F32), (0, pad)).reshape(rows, 1)

    head_of_col = jnp.arange(width) // SSD_HEAD_DIM
    expand = (jnp.arange(rows)[:, None] == head_of_col[None, :]).astype(BF16)
    d_exp = jnp.repeat(d_skip.astype(F32), SSD_HEAD_DIM).reshape(1, width)
    cwx, cwb, cwc = conv_w[:, :width], conv_w[:, width:width + bcw], conv_w[:, width + bcw:]
    cbx = conv_b[:width].reshape(1, width)
    cbb = conv_b[width:width + bcw].reshape(1, bcw)
    cbc = conv_b[width + bcw:].reshape(1, bcw)

    def full(shape):
        return pl.BlockSpec(shape, lambda b, t: (0,) * len(shape))

    kern = functools.partial(_ssd_kernel, tile=tile, heads=heads)
    return pl.pallas_call(
        kern,
        out_shape=jax.ShapeDtypeStruct((bsz * seq, width), BF16),
        grid=(bsz, nt),
        in_specs=[pl.BlockSpec((tile, width), lambda b, t: (b * nt + t, z_blk)),
                  pl.BlockSpec((tile, width), lambda b, t: (b * nt + t, xs_blk)),
                  pl.BlockSpec((tile, bcw), lambda b, t: (b * nt + t, b_blk)),
                  pl.BlockSpec((tile, bcw), lambda b, t: (b * nt + t, c_blk)),
                  pl.BlockSpec((rows, tile), lambda b, t: (0, b * nt + t)),
                  full((CONV_WIDTH, width)), full((CONV_WIDTH, bcw)), full((CONV_WIDTH, bcw)),
                  full((1, width)), full((1, bcw)), full((1, bcw)),
                  full((rows, 1)), full((rows, 1)), full((1, width)), full((1, width)),
                  full((rows, width))],
        out_specs=pl.BlockSpec((tile, width), lambda b, t: (b * nt + t, 0)),
        scratch_shapes=[pltpu.VMEM((tile + CONV_HALO, width), F32),
                        pltpu.VMEM((tile + CONV_HALO, bcw), F32),
                        pltpu.VMEM((tile + CONV_HALO, bcw), F32),
                        pltpu.VMEM((tile + 8, max(width, bcw)), F32),
                        pltpu.VMEM((SSD_GROUPS, SSD_STATE, width // SSD_GROUPS), F32)],
        compiler_params=pltpu.CompilerParams(dimension_semantics=("arbitrary", "arbitrary"),
                                             vmem_limit_bytes=VMEM_LIMIT_V7X),
        name="ssd",
    )(proj, proj, proj, proj, gt, cwx, cwb, cwc, cbx, cbb, cbc, col(dt_bias), col(a_log),
      d_exp, norm_w.reshape(1, width), expand)


def _unit_lower_inverses(n_list, r, c):
    eye = (r == c).astype(F32)
    same2 = (r >> 1) == (c >> 1)
    ts = [eye - jnp.where(same2, n, 0.0) for n in n_list]
    s = 2
    while s < CHUNK:
        sh = s.bit_length()
        in_block = (r >> sh) == (c >> sh)
        low_left = in_block & ((r & (2 * s - 1)) >= s) & ((c & (2 * s - 1)) < s)
        tbs = [t.astype(BF16) for t in ts]
        xs = [_dot(jnp.where(low_left, n, 0.0).astype(BF16), tb) for n, tb in zip(n_list, tbs)]
        ts = [t - _dot(tb, x.astype(BF16)) for t, tb, x in zip(ts, tbs, xs)]
        s *= 2
    return ts


def _gdn_kernel(q_ref, k_ref, v_ref, z_ref, g_ref, cwq_ref, cwk_ref, cwv_ref, dtb_ref, alog_ref,
                nw_ref, o_ref, bufq, bufk, bufv, ubuf, s_ref, *, tile, hg, scale):
    L = CHUNK
    D = HEAD_DIM

    @pl.when(pl.program_id(2) == 0)
    def _():
        bufq[0:CONV_HALO, :] = jnp.zeros((CONV_HALO, bufq.shape[1]), F32)
        bufk[0:CONV_HALO, :] = jnp.zeros((CONV_HALO, bufk.shape[1]), F32)
        bufv[0:CONV_HALO, :] = jnp.zeros((CONV_HALO, bufv.shape[1]), F32)
        s_ref[...] = jnp.zeros(s_ref.shape, F32)

    q_all = _conv_silu(bufq, ubuf, q_ref, cwq_ref, None, tile)
    k_all = _conv_silu(bufk, ubuf, k_ref, cwk_ref, None, tile)
    v_all = _conv_silu(bufv, ubuf, v_ref, cwv_ref, None, tile)

    gl = g_ref[...]
    rows8 = lax.broadcasted_iota(jnp.int32, gl.shape, 0)
    decay_log = -jnp.exp(alog_ref[0]) * _softplus(gl + dtb_ref[0])
    gates = jnp.where(rows8 < 4, decay_log, jax.nn.sigmoid(gl))
    gates = jnp.concatenate([gates, jnp.zeros_like(gates)], axis=0)

    r, c = _iota2((L, L))
    incl = r >= c
    strict = r > c
    tri_u = (r <= c).astype(BF16)
    tri_eye = jnp.concatenate([incl.astype(BF16), (r == c).astype(BF16)], axis=0)

    nchunk = tile // L
    cols_c = [_dot3_nt(tri_eye, gates[:, ci * L:(ci + 1) * L]) for ci in range(nchunk)]
    rows_c = [_dot3(gates[:, ci * L:(ci + 1) * L], tri_u) for ci in range(nchunk)]

    qn, kn, vh = [], [], []
    for h in range(hg):
        hs = slice(h * D, (h + 1) * D)
        qh = q_all[:, hs]
        kh = k_all[:, hs]
        qn.append(qh * (lax.rsqrt(jnp.sum(qh * qh, axis=-1, keepdims=True) + EPS) * scale))
        kn.append(kh * lax.rsqrt(jnp.sum(kh * kh, axis=-1, keepdims=True) + EPS))
        vh.append(v_all[:, hs])

    units = [(h, ci) for ci in range(nchunk) for h in range(hg)]
    loc = {}
    n_list = []
    for (h, ci) in units:
        sl = slice(ci * L, (ci + 1) * L)
        gcc = cols_c[ci][0:L, h:h + 1]
        gcr = rows_c[ci][h:h + 1, :]
        beta = cols_c[ci][L:2 * L, 4 + h:5 + h]
        g_last = gcc[L - 1:L, :]
        decay = jnp.exp(jnp.where(incl, gcc - gcr, NEG_BIG))
        kc = kn[h][sl]
        qc = qn[h][sl]
        kc_b = kc.astype(BF16)
        k_beta = kc * beta
        kk = _dot_nt(k_beta.astype(BF16), kc_b) * decay
        n_list.append(jnp.where(strict, kk, 0.0))
        rhs = jnp.concatenate([vh[h][sl] * beta, k_beta * jnp.exp(gcc)], axis=1)
        loc[(h, ci)] = dict(
            rhs=rhs.astype(BF16),
            qk=(_dot_nt(qc.astype(BF16), kc_b) * decay).astype(BF16),
            k_tail=(kc * jnp.exp(g_last - gcc)).astype(BF16),
            q_dec=(qc * jnp.exp(gcc)).astype(BF16),
            e_last=jnp.exp(g_last))
    t_invs = _unit_lower_inverses(n_list, r, c)
    for unit, t_inv in zip(units, t_invs):
        loc[unit]["sol"] = _dot(t_inv.astype(BF16), loc[unit]["rhs"])

    states = [s_ref[h] for h in range(hg)]
    outs = {}
    for ci in range(nchunk):
        sb = [st.astype(BF16) for st in states]
        v_new = []
        for h in range(hg):
            sol = loc[(h, ci)]["sol"]
            v_new.append((sol[:, 0:D] - _dot(sol[:, D:2 * D].astype(BF16), sb[h])).astype(BF16))
        for h in range(hg):
            u = loc[(h, ci)]
            outs[(h, ci)] = _dot(u["q_dec"], sb[h]) + _dot(u["qk"], v_new[h])
            states[h] = states[h] * u["e_last"] + _dot_tn(u["k_tail"], v_new[h])
    for h in range(hg):
        s_ref[h] = states[h]

    rows = []
    for ci in range(nchunk):
        heads_out = []
        for h in range(hg):
            o = outs[(h, ci)]
            ms = jnp.mean(o * o, axis=-1, keepdims=True)
            heads_out.append(o * lax.rsqrt(ms + EPS) * nw_ref[...])
        rows.append(jnp.concatenate(heads_out, axis=1))
    o_all = jnp.concatenate(rows, axis=0) if nchunk > 1 else rows[0]
    o_ref[...] = (o_all * _silu(z_ref[...].astype(F32))).astype(o_ref.dtype)


def _gdn(proj, gt, conv_w, dt_bias, a_log, norm_w, bsz, seq, heads, hg, q_blk, k_blk, v_blk,
         z_blk, gate_blk):
    d = HEAD_DIM
    width = heads * d
    gwid = hg * d
    ngrp = heads // hg
    tile = _pick_tile(seq, (256, 128, 64))
    nt = seq // tile
    cwq, cwk, cwv = conv_w[:, :width], conv_w[:, width:2 * width], conv_w[:, 2 * width:]

    def grp_col(v):
        v = jnp.pad(v.astype(F32).reshape(ngrp, hg), ((0, 0), (0, GDN_GATE_ROWS - hg)))
        return v.reshape(ngrp, GDN_GATE_ROWS, 1)

    def tok(blk):
        return pl.BlockSpec((tile, gwid), lambda b, g, t: (b * nt + t, blk + g))

    def cw():
        return pl.BlockSpec((CONV_WIDTH, gwid), lambda b, g, t: (0, g))

    def gcol():
        return pl.BlockSpec((1, GDN_GATE_ROWS, 1), lambda b, g, t: (g, 0, 0))

    kern = functools.partial(_gdn_kernel, tile=tile, hg=hg, scale=d ** -0.5)
    return pl.pallas_call(
        kern,
        out_shape=jax.ShapeDtypeStruct((bsz * seq, width), BF16),
        grid=(bsz, ngrp, nt),
        in_specs=[tok(q_blk), tok(k_blk), tok(v_blk), tok(z_blk),
                  pl.BlockSpec((GDN_GATE_ROWS, tile), lambda b, g, t: (gate_blk + g, b * nt + t)),
                  cw(), cw(), cw(), gcol(), gcol(),
                  pl.BlockSpec((1, d), lambda b, g, t: (0, 0))],
        out_specs=pl.BlockSpec((tile, gwid), lambda b, g, t: (b * nt + t, g)),
        scratch_shapes=[pltpu.VMEM((tile + CONV_HALO, gwid), F32),
                        pltpu.VMEM((tile + CONV_HALO, gwid), F32),
                        pltpu.VMEM((tile + CONV_HALO, gwid), F32),
                        pltpu.VMEM((tile + 8, gwid), F32),
                        pltpu.VMEM((hg, d, d), F32)],
        compiler_params=pltpu.CompilerParams(
            dimension_semantics=("arbitrary", "arbitrary", "arbitrary"),
            vmem_limit_bytes=VMEM_LIMIT_V7X),
        name="gdn",
    )(proj, proj, proj, proj, gt, cwq, cwk, cwv, grp_col(dt_bias), grp_col(a_log),
      norm_w.reshape(1, d))


def _out_proj_kernel(x_ref, mf_ref, ms_ref, mg_ref, wf_ref, ws_ref, wg_ref, o_ref):
    acc = _dot(mf_ref[...], wf_ref[...])
    acc = acc + _dot(ms_ref[...], ws_ref[...])
    acc = acc + _dot(mg_ref[...], wg_ref[...])
    o_ref[...] = x_ref[...] + acc


def _out_proj(x2d, mf, ms, mg, wf, ws, wg):
    m, d = x2d.shape
    tm = _pick_tile(m, (1024, 512, 256, 128))
    tn = _pick_tile(d, (1024, 512, 256, 128))

    def act(a):
        return pl.BlockSpec((tm, a.shape[1]), lambda i, j: (i, 0))

    def wgt(a):
        return pl.BlockSpec((a.shape[0], tn), lambda i, j: (0, j))

    return pl.pallas_call(
        _out_proj_kernel,
        out_shape=jax.ShapeDtypeStruct((m, d), F32),
        grid=(m // tm, d // tn),
        in_specs=[pl.BlockSpec((tm, tn), lambda i, j: (i, j)),
                  act(mf), act(ms), act(mg), wgt(wf), wgt(ws), wgt(wg)],
        out_specs=pl.BlockSpec((tm, tn), lambda i, j: (i, j)),
        compiler_params=pltpu.CompilerParams(dimension_semantics=("arbitrary", "arbitrary"),
                                             vmem_limit_bytes=VMEM_LIMIT_V7X),
        name="out_proj",
    )(x2d, mf, ms, mg, wf, ws, wg)


def _layer(x2d, bsz, seq, norm_w, w_in, w_out, fox_b_f, fox_q_norm_w, fox_k_norm_w,
           fox_out_norm_w, ssd_conv_w, ssd_conv_b, ssd_dt_bias, ssd_a_log, ssd_d, ssd_norm_w,
           gdn_conv_w, gdn_dt_bias, gdn_a_log, gdn_norm_w):
    d_model = x2d.shape[1]
    fox_w = d_model // 4
    ssd_w = 3 * d_model // 8
    gdn_w = d_model - fox_w - ssd_w
    fox_h = fox_w // HEAD_DIM
    ssd_h = ssd_w // SSD_HEAD_DIM
    gdn_h = gdn_w // HEAD_DIM
    bcw = SSD_GROUPS * SSD_STATE
    hg = 4 if gdn_h % 4 == 0 else 2
    ngrp = gdn_h // hg
    assert fox_h <= FOX_GATE_ROWS and ssd_h <= SSD_GATE_ROWS and ssd_h % SSD_GROUPS == 0
    assert gdn_h % hg == 0 and seq % CHUNK == 0

    sizes = (3 * fox_w, fox_h, fox_w, ssd_w + 2 * bcw, ssd_w, ssd_h, 3 * gdn_w, gdn_w, gdn_h, gdn_h)
    offs = [0]
    for s in sizes:
        offs.append(offs[-1] + s)
    (o_fqkv, o_ff, o_fz, o_sxbc, o_sz, o_sdt, o_gqkv, o_gz, o_gb, o_ga) = offs[:-1]

    def cols(o, n):
        return w_in[:, o:o + n].astype(BF16)

    w_main = jnp.concatenate(
        [cols(o_sz, ssd_w), cols(o_sxbc, ssd_w + 2 * bcw), cols(o_fqkv, 3 * fox_w),
         cols(o_fz, fox_w), cols(o_gqkv, 3 * gdn_w), cols(o_gz, gdn_w)], axis=1)
    c_sz, c_sxs, c_sb = 0, ssd_w, 2 * ssd_w
    c_sc = c_sb + bcw
    c_fq = c_sc + bcw
    c_fz = c_fq + 3 * fox_w
    c_gq = c_fz + fox_w
    c_gz = c_gq + 3 * gdn_w
    gwid = hg * HEAD_DIM
    assert c_sb % bcw == 0 and c_fq % HEAD_DIM == 0 and c_gq % gwid == 0 and gdn_w % gwid == 0

    def rows(o, n, total):
        return jnp.pad(w_in[:, o:o + n].T, ((0, total - n), (0, 0)))

    gate_rows = [rows(o_sdt, ssd_h, SSD_GATE_ROWS), rows(o_ff, fox_h, FOX_GATE_ROWS)]
    for g in range(ngrp):
        gate_rows.append(rows(o_ga + g * hg, hg, 4))
        gate_rows.append(rows(o_gb + g * hg, hg, 4))
    wgt = jnp.concatenate(gate_rows, axis=0).astype(BF16)
    fox_row_blk = SSD_GATE_ROWS // FOX_GATE_ROWS
    gdn_row_blk = (SSD_GATE_ROWS + FOX_GATE_ROWS) // GDN_GATE_ROWS

    h, gt = _prenorm(x2d, norm_w, wgt)
    proj = _in_proj(h, w_main)

    b_f = jnp.pad(fox_b_f.astype(F32), (0, FOX_GATE_ROWS - fox_h)).reshape(FOX_GATE_ROWS, 1)
    f_cum = _fox_gate(gt, b_f, bsz, seq, fox_row_blk)
    f3 = f_cum.reshape(bsz * FOX_GATE_ROWS, 1, seq)
    mix_fox = _fox_attn(proj, f3, fox_q_norm_w, fox_k_norm_w, fox_out_norm_w, bsz, seq, fox_h,
                        c_fq // HEAD_DIM, (c_fq + fox_w) // HEAD_DIM,
                        (c_fq + 2 * fox_w) // HEAD_DIM, c_fz // HEAD_DIM)
    mix_ssd = _ssd(proj, gt, ssd_conv_w, ssd_conv_b, ssd_dt_bias, ssd_a_log, ssd_d, ssd_norm_w,
                   bsz, seq, ssd_h, c_sz // ssd_w, c_sxs // ssd_w, c_sb // bcw, c_sc // bcw)
    mix_gdn = _gdn(proj, gt, gdn_conv_w, gdn_dt_bias, gdn_a_log, gdn_norm_w, bsz, seq, gdn_h, hg,
                   c_gq // gwid, (c_gq + gdn_w) // gwid, (c_gq + 2 * gdn_w) // gwid,
                   c_gz // gwid, gdn_row_blk)

    w_o = w_out.astype(BF16)
    return _out_proj(x2d, mix_fox, mix_ssd, mix_gdn, w_o[:fox_w], w_o[fox_w:fox_w + ssd_w],
                     w_o[fox_w + ssd_w:])


def kernel(x, norm_w, w_in, w_out, fox_b_f, fox_q_norm_w, fox_k_norm_w, fox_out_norm_w, ssd_conv_w, ssd_conv_b, ssd_dt_bias, ssd_A_log, ssd_D, ssd_norm_w, gdn_conv_w, gdn_dt_bias, gdn_A_log, gdn_norm_w):
    bsz, seq, d_model = x.shape
    x2d = x.reshape(bsz * seq, d_model)
    for l in range(norm_w.shape[0]):
        x2d = _layer(x2d, bsz, seq, norm_w[l], w_in[l], w_out[l], fox_b_f[l], fox_q_norm_w[l],
                     fox_k_norm_w[l], fox_out_norm_w[l], ssd_conv_w[l], ssd_conv_b[l],
                     ssd_dt_bias[l], ssd_A_log[l], ssd_D[l], ssd_norm_w[l], gdn_conv_w[l],
                     gdn_dt_bias[l], gdn_A_log[l], gdn_norm_w[l])
    return x2d.reshape(bsz, seq, d_model)
```

```python
import functools

import jax
import jax.numpy as jnp
from jax import lax
from jax.experimental import pallas as pl
from jax.experimental.pallas import tpu as pltpu

F32 = jnp.float32
BF16 = jnp.bfloat16

EPS = 1e-6
CHUNK = 64
HEAD_DIM = 128
SSD_HEAD_DIM = 64
SSD_GROUPS = 4
SSD_STATE = 128
CONV_WIDTH = 4
CONV_HALO = 16
SSD_GATE_ROWS = 32
FOX_GATE_ROWS = 8
GDN_GATE_ROWS = 8
NEG_BIG = -1e30
VMEM_LIMIT_V7X = 56 * 1024 * 1024


def _dot(a, b):
    return jnp.dot(a, b, preferred_element_type=F32)


def _dot_nt(a, b):
    return lax.dot_general(a, b, (((1,), (1,)), ((), ())), preferred_element_type=F32)


def _dot_tn(a, b):
    return lax.dot_general(a, b, (((0,), (0,)), ((), ())), preferred_element_type=F32)


def _split3(a):
    a1 = a.astype(BF16)
    r1 = a - a1.astype(F32)
    a2 = r1.astype(BF16)
    r2 = r1 - a2.astype(F32)
    return a1, a2, r2.astype(BF16)


def _dot3(a, m):
    p1, p2, p3 = _split3(a)
    return _dot(p1, m) + _dot(p2, m) + _dot(p3, m)


def _dot3_nt(m, a):
    p1, p2, p3 = _split3(a)
    return _dot_nt(m, p1) + _dot_nt(m, p2) + _dot_nt(m, p3)


def _silu(x):
    return x * jax.nn.sigmoid(x)


def _softplus(x):
    return jnp.maximum(x, 0.0) + jnp.log1p(jnp.exp(-jnp.abs(x)))


def _log_sigmoid(x):
    return jnp.minimum(x, 0.0) - jnp.log1p(jnp.exp(-jnp.abs(x)))


def _iota2(shape):
    return (lax.broadcasted_iota(jnp.int32, shape, 0), lax.broadcasted_iota(jnp.int32, shape, 1))


def _pick_tile(n, candidates):
    for c in candidates:
        if n % c == 0 and n // c >= 2:
            return c
    return n


def _prenorm_kernel(x_ref, nw_ref, wg_ref, h_ref, gt_ref, wgt_ref):
    @pl.when(pl.program_id(0) == 0)
    def _():
        wgt_ref[...] = jnp.transpose(wg_ref[...].astype(F32)).astype(BF16)

    x = x_ref[...]
    ms = jnp.mean(x * x, axis=-1, keepdims=True)
    hb = (x * lax.rsqrt(ms + EPS) * nw_ref[...]).astype(BF16)
    h_ref[...] = hb
    gt_ref[...] = _dot_nt(wgt_ref[0:gt_ref.shape[0], :], hb)


def _prenorm(x2d, norm_w, wg, gate_rows):
    m, d = x2d.shape
    gcols = wg.shape[1]
    tm = _pick_tile(m, (512, 256, 128))
    return pl.pallas_call(
        _prenorm_kernel,
        out_shape=(jax.ShapeDtypeStruct((m, d), BF16), jax.ShapeDtypeStruct((gate_rows, m), F32)),
        grid=(m // tm,),
        in_specs=[pl.BlockSpec((tm, d), lambda i: (i, 0)),
                  pl.BlockSpec((1, d), lambda i: (0, 0)),
                  pl.BlockSpec((d, gcols), lambda i: (0, 0))],
        out_specs=(pl.BlockSpec((tm, d), lambda i: (i, 0)),
                   pl.BlockSpec((gate_rows, tm), lambda i: (0, i))),
        scratch_shapes=[pltpu.VMEM((gcols, d), BF16)],
        compiler_params=pltpu.CompilerParams(dimension_semantics=("arbitrary",),
                                             vmem_limit_bytes=VMEM_LIMIT_V7X),
        name="prenorm_gates",
    )(x2d, norm_w.reshape(1, d), wg)


def _in_proj_kernel(h_ref, w_ref, o_ref):
    o_ref[...] = _dot(h_ref[...], w_ref[...]).astype(o_ref.dtype)


def _in_proj(h, w):
    m, d = h.shape
    n = w.shape[1]
    tm = _pick_tile(m, (1024, 512, 256, 128))
    tn = _pick_tile(n, (1024, 512, 256, 128))
    return pl.pallas_call(
        _in_proj_kernel,
        out_shape=jax.ShapeDtypeStruct((m, n), BF16),
        grid=(m // tm, n // tn),
        in_specs=[pl.BlockSpec((tm, d), lambda i, j: (i, 0)),
                  pl.BlockSpec((d, tn), lambda i, j: (0, j))],
        out_specs=pl.BlockSpec((tm, tn), lambda i, j: (i, j)),
        compiler_params=pltpu.CompilerParams(dimension_semantics=("arbitrary", "arbitrary"),
                                             vmem_limit_bytes=VMEM_LIMIT_V7X),
        name="in_proj",
    )(h, w)


def _fox_gate_kernel(g_ref, b_ref, f_ref):
    seq = g_ref.shape[1]
    lf = _log_sigmoid(g_ref[...] + b_ref[...])
    r, c = _iota2((128, 128))
    upper = (r <= c).astype(BF16)
    carry = jnp.zeros((lf.shape[0], 1), F32)
    for i in range(seq // 128):
        cs = _dot3(lf[:, i * 128:(i + 1) * 128], upper) + carry
        f_ref[0, :, i * 128:(i + 1) * 128] = cs
        carry = cs[:, 127:128]


def _fox_gate(gt, b_f, bsz, seq, row_block):
    rows = FOX_GATE_ROWS
    return pl.pallas_call(
        _fox_gate_kernel,
        out_shape=jax.ShapeDtypeStruct((bsz, rows, seq), F32),
        grid=(bsz,),
        in_specs=[pl.BlockSpec((rows, seq), lambda b: (row_block, b)),
                  pl.BlockSpec((rows, 1), lambda b: (0, 0))],
        out_specs=pl.BlockSpec((1, rows, seq), lambda b: (b, 0, 0)),
        compiler_params=pltpu.CompilerParams(dimension_semantics=("arbitrary",)),
        name="fox_gate",
    )(gt, b_f)


LOG2E = 1.4426950408889634
FOX_SPLIT = 2
FOX_VROWS = 144


def _fox_attn_kernel(q_ref, k_ref, v_ref, z_ref, f_ref, wq_ref, wk_ref, wo_ref, o_ref,
                     ka_ref, vat_ref, qa_ref, m_ref, acc_ref, s_ref, *, tq, scale):
    qi = pl.program_id(2)
    seq = k_ref.shape[0]
    d = HEAD_DIM
    tw = tq // FOX_SPLIT
    lane = lax.broadcasted_iota(jnp.int32, (tq, d), 1)

    @pl.when(qi == 0)
    def _():
        row8 = lax.broadcasted_iota(jnp.int32, (8, tq), 0)
        ones_row = (lax.broadcasted_iota(jnp.int32, (FOX_VROWS - d, tq), 0) == 0).astype(BF16)

        def body(i, carry):
            r0 = pl.multiple_of(i * tq, tq)
            kk = k_ref[pl.ds(r0, tq), :].astype(F32)
            ms = jnp.mean(kk * kk, axis=-1, keepdims=True)
            ka_ref[pl.ds(r0, tq), 0:d] = (kk * lax.rsqrt(ms + EPS) * wk_ref[...]).astype(BF16)
            frow = f_ref[0, :, pl.ds(r0, tq)]
            b1, b2, b3 = _split3((frow[:, 0:1] - frow) * LOG2E)
            parts = jnp.where(row8 == 0, b1.astype(F32),
                              jnp.where(row8 == 1, b2.astype(F32),
                                        jnp.where(row8 == 2, b3.astype(F32), 0.0)))
            parts = jnp.concatenate([parts, jnp.zeros((d - 8, tq), F32)], axis=0)
            ka_ref[pl.ds(r0, tq), d:2 * d] = jnp.transpose(parts).astype(BF16)
            vt = jnp.transpose(v_ref[pl.ds(r0, tq), :].astype(F32))
            vat_ref[i, 0:d, :] = vt.astype(BF16)
            vat_ref[i, d:FOX_VROWS, :] = ones_row
            return carry
        lax.fori_loop(0, seq // tq, body, 0)

    q = q_ref[...].astype(F32)
    ms = jnp.mean(q * q, axis=-1, keepdims=True)
    qa_ref[:, 0:d] = (q * lax.rsqrt(ms + EPS) * (wq_ref[...] * (scale * LOG2E))).astype(BF16)
    qa_ref[:, d:2 * d] = (lane < 3).astype(BF16)
    q0 = pl.multiple_of(qi * tq, tq)
    f_first = f_ref[0, :, pl.ds(q0, tq)][:, 0:1]

    m_ref[...] = jnp.full(m_ref.shape, NEG_BIG, F32)
    acc_ref[...] = jnp.zeros(acc_ref.shape, F32)

    def scores_to(slot, kb):
        r0 = pl.multiple_of(kb * tq, tq)
        s_ref[slot] = _dot_nt(ka_ref[pl.ds(r0, tq), :], qa_ref[...])

    def consume(kb, slot, masked):
        r0 = pl.multiple_of(kb * tq, tq)
        shift = (f_first - f_ref[0, :, pl.ds(r0, tq)][:, 0:1]) * LOG2E
        new_m, new_acc = [], []
        for h in range(FOX_SPLIT):
            cols = slice(h * tw, (h + 1) * tw)
            nk = (h + 1) * tw if masked else tq
            s = s_ref[slot, 0:nk, cols]
            if masked:
                r, c = _iota2((nk, tw))
                s = jnp.where(r <= c + h * tw, s, NEG_BIG)
            m_prev = m_ref[:, cols]
            m_new = jnp.maximum(m_prev, jnp.max(s, axis=0, keepdims=True) + shift)
            p = jnp.exp2(s - (m_new - shift)).astype(BF16)
            alpha = jnp.exp2(m_prev - m_new)
            pv = _dot(vat_ref[kb, :, 0:nk], p)
            new_acc.append(alpha * acc_ref[:, cols] + pv)
            new_m.append(m_new)
        m_ref[...] = jnp.concatenate(new_m, axis=1)
        acc_ref[...] = jnp.concatenate(new_acc, axis=1)

    scores_to(0, 0)

    def pair_body(j, carry):
        kb = 2 * j
        scores_to(1, kb + 1)
        consume(kb, 0, False)
        scores_to(0, kb + 2)
        consume(kb + 1, 1, False)
        return carry
    lax.fori_loop(0, qi // 2, pair_body, 0)

    @pl.when(qi % 2 == 1)
    def _():
        scores_to(1, qi)
        consume(qi - 1, 0, False)
        consume(qi, 1, True)

    @pl.when(qi % 2 == 0)
    def _():
        consume(qi, 0, True)

    o = jnp.transpose(acc_ref[0:d, :] / acc_ref[d:d + 1, :])
    ms = jnp.mean(o * o, axis=-1, keepdims=True)
    o = o * lax.rsqrt(ms + EPS) * wo_ref[...]
    o_ref[...] = (o * _silu(z_ref[...].astype(F32))).astype(o_ref.dtype)


def _fox_attn(proj, f3, wq, wk, wo, bsz, seq, heads, q_blk, k_blk, v_blk, z_blk):
    d = HEAD_DIM
    tq = _pick_tile(seq, (512, 256, 128))
    nq = seq // tq
    kern = functools.partial(_fox_attn_kernel, tq=tq, scale=d ** -0.5)
    vec = pl.BlockSpec((1, d), lambda b, h, i: (0, 0))
    return pl.pallas_call(
        kern,
        out_shape=jax.ShapeDtypeStruct((bsz * seq, heads * d), BF16),
        grid=(bsz, heads, nq),
        in_specs=[pl.BlockSpec((tq, d), lambda b, h, i: (b * nq + i, q_blk + h)),
                  pl.BlockSpec((seq, d), lambda b, h, i: (b, k_blk + h)),
                  pl.BlockSpec((seq, d), lambda b, h, i: (b, v_blk + h)),
                  pl.BlockSpec((tq, d), lambda b, h, i: (b * nq + i, z_blk + h)),
                  pl.BlockSpec((1, 1, seq), lambda b, h, i: (b * FOX_GATE_ROWS + h, 0, 0)),
                  vec, vec, vec],
        out_specs=pl.BlockSpec((tq, d), lambda b, h, i: (b * nq + i, h)),
        scratch_shapes=[pltpu.VMEM((seq, 2 * d), BF16),
                        pltpu.VMEM((nq, FOX_VROWS, tq), BF16),
                        pltpu.VMEM((tq, 2 * d), BF16),
                        pltpu.VMEM((1, tq), F32),
                        pltpu.VMEM((FOX_VROWS, tq), F32),
                        pltpu.VMEM((2, tq, tq), F32)],
        compiler_params=pltpu.CompilerParams(
            dimension_semantics=("arbitrary", "arbitrary", "arbitrary"),
            vmem_limit_bytes=VMEM_LIMIT_V7X),
        name="fox_attn",
    )(proj, proj, proj, proj, f3, wq.reshape(1, d), wk.reshape(1, d), wo.reshape(1, d))


def _conv_silu(buf, ubuf, src_ref, w_ref, bias, tile):
    assert CONV_WIDTH == 4 and CONV_HALO >= 16
    h0 = CONV_HALO
    width = buf.shape[1]
    buf[h0:h0 + tile, :] = src_ref[...].astype(F32)
    x = buf[h0 - 8:h0 + tile, :]
    z = buf[h0 - 9:h0 + tile - 1, :]
    ubuf[0:tile + 8, 0:width] = w_ref[1:2, :] * x + w_ref[0:1, :] * z
    acc = w_ref[3:4, :] * x[8:] + w_ref[2:3, :] * z[8:] + ubuf[6:tile + 6, 0:width]
    if bias is not None:
        acc = acc + bias
    buf[0:h0, :] = buf[tile:tile + h0, :]
    return _silu(acc)


def _ssd_kernel(z_ref, xs_ref, b_ref, c_ref, g_ref, cwx_ref, cwb_ref, cwc_ref,
                cbx_ref, cbb_ref, cbc_ref, dtb_ref, alog_ref, dexp_ref, nw_ref, e_ref,
                o_ref, bufx, bufb, bufc, ubuf, ht_ref, *, tile, heads):
    L = CHUNK
    P = SSD_HEAD_DIM
    N = SSD_STATE
    per_group = heads // SSD_GROUPS
    gw = per_group * P
    nchunk = tile // L

    @pl.when(pl.program_id(1) == 0)
    def _():
        bufx[0:CONV_HALO, :] = jnp.zeros((CONV_HALO, bufx.shape[1]), F32)
        bufb[0:CONV_HALO, :] = jnp.zeros((CONV_HALO, bufb.shape[1]), F32)
        bufc[0:CONV_HALO, :] = jnp.zeros((CONV_HALO, bufc.shape[1]), F32)
        ht_ref[...] = jnp.zeros(ht_ref.shape, F32)

    xs = _conv_silu(bufx, ubuf, xs_ref, cwx_ref, cbx_ref[...], tile)
    bm = _conv_silu(bufb, ubuf, b_ref, cwb_ref, cbb_ref[...], tile)
    cm = _conv_silu(bufc, ubuf, c_ref, cwc_ref, cbc_ref[...], tile)
    bm_b = bm.astype(BF16)
    cm_b = cm.astype(BF16)

    dt_row = _softplus(g_ref[...] + dtb_ref[...])
    a_row = dt_row * (-jnp.exp(alog_ref[...]))

    r, c = _iota2((L, L))
    tril = r >= c
    tri_u = (r <= c).astype(BF16)
    tri_l = tril.astype(BF16)
    eye = (r == c).astype(BF16)
    expand = e_ref[...]
    first_half = lax.broadcasted_iota(jnp.int32, (L, 2 * P), 1) < P

    chunks = []
    for ci in range(nchunk):
        sl = slice(ci * L, (ci + 1) * L)
        a_c = a_row[:, sl]
        cs_row = _dot3(a_c, tri_u)
        cs_col = _dot3_nt(tri_l, a_c)
        dt_col = _dot3_nt(eye, dt_row[:, sl])
        total = cs_col[L - 1:L, :]
        ds_col = jnp.exp(total - cs_col)
        ecs_col = jnp.exp(cs_col)
        cd_row = jnp.broadcast_to(jnp.exp(total), (8, total.shape[1]))
        stacked = jnp.concatenate([dt_col, ds_col, ecs_col, cd_row], axis=0)
        s1 = stacked.astype(BF16)
        s2 = (stacked - s1.astype(F32)).astype(BF16)
        ex = _dot(s1, expand) + _dot(s2, expand)
        xs_c = xs[sl]
        xdt = xs_c * ex[0:L]
        xdt_b = xdt.astype(BF16)
        cbs = [_dot_nt(cm_b[sl, g * N:(g + 1) * N], bm_b[sl, g * N:(g + 1) * N])
               for g in range(SSD_GROUPS)]
        y_pairs = []
        for hp in range(heads // 2):
            xpair = xdt_b[:, hp * 2 * P:(hp + 1) * 2 * P]
            acc = None
            for hd, keep in ((2 * hp, first_half), (2 * hp + 1, ~first_half)):
                seg = cs_col[:, hd:hd + 1] - cs_row[hd:hd + 1, :]
                lm = jnp.exp(jnp.where(tril, seg, NEG_BIG))
                part = _dot((cbs[hd // per_group] * lm).astype(BF16),
                            jnp.where(keep, xpair, jnp.zeros_like(xpair)))
                acc = part if acc is None else acc + part
            y_pairs.append(acc)
        chunks.append(dict(
            y_diag=jnp.concatenate(y_pairs, axis=1),
            xw=(xdt * ex[L:2 * L]).astype(BF16),
            ecs=ex[2 * L:3 * L], cd=ex[3 * L:3 * L + 1], xs=xs_c))

    hts = [ht_ref[g] for g in range(SSD_GROUPS)]
    rows_out = []
    for ci in range(nchunk):
        sl = slice(ci * L, (ci + 1) * L)
        ck = chunks[ci]
        y_off = []
        for g in range(SSD_GROUPS):
            cols = slice(g * gw, (g + 1) * gw)
            y_off.append(_dot(cm_b[sl, g * N:(g + 1) * N], hts[g].astype(BF16)) * ck["ecs"][:, cols])
            hts[g] = hts[g] * ck["cd"][:, cols] + _dot_tn(bm_b[sl, g * N:(g + 1) * N], ck["xw"][:, cols])
        y = ck["y_diag"] + jnp.concatenate(y_off, axis=1)
        y = (y + ck["xs"] * dexp_ref[...]) * _silu(z_ref[sl, :].astype(F32))
        outs = []
        for g in range(SSD_GROUPS):
            yg = y[:, g * gw:(g + 1) * gw]
            ms = jnp.mean(yg * yg, axis=-1, keepdims=True)
            outs.append(yg * lax.rsqrt(ms + EPS))
        rows_out.append(jnp.concatenate(outs, axis=1) * nw_ref[...])
    for g in range(SSD_GROUPS):
        ht_ref[g] = hts[g]
    o_all = jnp.concatenate(rows_out, axis=0) if nchunk > 1 else rows_out[0]
    o_ref[...] = o_all.astype(o_ref.dtype)


def _ssd(proj, gt, conv_w, conv_b, dt_bias, a_log, d_skip, norm_w, bsz, seq, heads,
         z_blk, xs_blk, b_blk, c_blk):
    width = heads * SSD_HEAD_DIM
    bcw = SSD_GROUPS * SSD_STATE
    tile = _pick_tile(seq, (512, 256, 128, 64))
    nt = seq // tile
    rows = SSD_GATE_ROWS
    pad = rows - heads

    def col(v):
        return jnp.pad(v.astype(F32), (0, pad)).reshape(rows, 1)

    head_of_col = jnp.arange(width) // SSD_HEAD_DIM
    expand = (jnp.arange(rows)[:, None] == head_of_col[None, :]).astype(BF16)
    d_exp = jnp.repeat(d_skip.astype(F32), SSD_HEAD_DIM).reshape(1, width)
    cwx, cwb, cwc = conv_w[:, :width], conv_w[:, width:width + bcw], conv_w[:, width + bcw:]
    cbx = conv_b[:width].reshape(1, width)
    cbb = conv_b[width:width + bcw].reshape(1, bcw)
    cbc = conv_b[width + bcw:].reshape(1, bcw)

    def full(shape):
        return pl.BlockSpec(shape, lambda b, t: (0,) * len(shape))

    kern = functools.partial(_ssd_kernel, tile=tile, heads=heads)
    return pl.pallas_call(
        kern,
        out_shape=jax.ShapeDtypeStruct((bsz * seq, width), BF16),
        grid=(bsz, nt),
        in_specs=[pl.BlockSpec((tile, width), lambda b, t: (b * nt + t, z_blk)),
                  pl.BlockSpec((tile, width), lambda b, t: (b * nt + t, xs_blk)),
                  pl.BlockSpec((tile, bcw), lambda b, t: (b * nt + t, b_blk)),
                  pl.BlockSpec((tile, bcw), lambda b, t: (b * nt + t, c_blk)),
                  pl.BlockSpec((rows, tile), lambda b, t: (0, b * nt + t)),
                  full((CONV_WIDTH, width)), full((CONV_WIDTH, bcw)), full((CONV_WIDTH, bcw)),
                  full((1, width)), full((1, bcw)), full((1, bcw)),
                  full((rows, 1)), full((rows, 1)), full((1, width)), full((1, width)),
                  full((rows, width))],
        out_specs=pl.BlockSpec((tile, width), lambda b, t: (b * nt + t, 0)),
        scratch_shapes=[pltpu.VMEM((tile + CONV_HALO, width), F32),
                        pltpu.VMEM((tile + CONV_HALO, bcw), F32),
                        pltpu.VMEM((tile + CONV_HALO, bcw), F32),
                        pltpu.VMEM((tile + 8, max(width, bcw)), F32),
                        pltpu.VMEM((SSD_GROUPS, SSD_STATE, width // SSD_GROUPS), F32)],
        compiler_params=pltpu.CompilerParams(dimension_semantics=("arbitrary", "arbitrary"),
                                             vmem_limit_bytes=VMEM_LIMIT_V7X),
        name="ssd",
    )(proj, proj, proj, proj, gt, cwx, cwb, cwc, cbx, cbb, cbc, col(dt_bias), col(a_log),
      d_exp, norm_w.reshape(1, width), expand)


def _unit_lower_inverses(n_list, r, c):
    eye = (r == c).astype(F32)
    same2 = (r >> 1) == (c >> 1)
    ts = [eye - jnp.where(same2, n, 0.0) for n in n_list]
    s = 2
    while s < CHUNK:
        sh = s.bit_length()
        in_block = (r >> sh) == (c >> sh)
        low_left = in_block & ((r & (2 * s - 1)) >= s) & ((c & (2 * s - 1)) < s)
        tbs = [t.astype(BF16) for t in ts]
        xs = [_dot(jnp.where(low_left, n, 0.0).astype(BF16), tb) for n, tb in zip(n_list, tbs)]
        ts = [t - _dot(tb, x.astype(BF16)) for t, tb, x in zip(ts, tbs, xs)]
        s *= 2
    return ts


def _gdn_kernel(q_ref, k_ref, v_ref, z_ref, g_ref, cwq_ref, cwk_ref, cwv_ref, dtb_ref, alog_ref,
                nw_ref, o_ref, bufq, bufk, bufv, ubuf, s_ref, *, tile, hg, scale):
    L = CHUNK
    D = HEAD_DIM

    @pl.when(pl.program_id(2) == 0)
    def _():
        bufq[0:CONV_HALO, :] = jnp.zeros((CONV_HALO, bufq.shape[1]), F32)
        bufk[0:CONV_HALO, :] = jnp.zeros((CONV_HALO, bufk.shape[1]), F32)
        bufv[0:CONV_HALO, :] = jnp.zeros((CONV_HALO, bufv.shape[1]), F32)
        s_ref[...] = jnp.zeros(s_ref.shape, F32)

    q_all = _conv_silu(bufq, ubuf, q_ref, cwq_ref, None, tile)
    k_all = _conv_silu(bufk, ubuf, k_ref, cwk_ref, None, tile)
    v_all = _conv_silu(bufv, ubuf, v_ref, cwv_ref, None, tile)

    gl = g_ref[...]
    rows8 = lax.broadcasted_iota(jnp.int32, gl.shape, 0)
    decay_log = -jnp.exp(alog_ref[0]) * _softplus(gl + dtb_ref[0])
    gates = jnp.where(rows8 < 4, decay_log, jax.nn.sigmoid(gl))
    gates = jnp.concatenate([gates, jnp.zeros_like(gates)], axis=0)

    r, c = _iota2((L, L))
    incl = r >= c
    strict = r > c
    tri_u = (r <= c).astype(BF16)
    tri_eye = jnp.concatenate([incl.astype(BF16), (r == c).astype(BF16)], axis=0)

    nchunk = tile // L
    cols_c = [_dot3_nt(tri_eye, gates[:, ci * L:(ci + 1) * L]) for ci in range(nchunk)]
    rows_c = [_dot3(gates[:, ci * L:(ci + 1) * L], tri_u) for ci in range(nchunk)]

    qn, kn, vh = [], [], []
    for h in range(hg):
        hs = slice(h * D, (h + 1) * D)
        qh = q_all[:, hs]
        kh = k_all[:, hs]
        qn.append(qh * (lax.rsqrt(jnp.sum(qh * qh, axis=-1, keepdims=True) + EPS) * scale))
        kn.append(kh * lax.rsqrt(jnp.sum(kh * kh, axis=-1, keepdims=True) + EPS))
        vh.append(v_all[:, hs])

    units = [(h, ci) for ci in range(nchunk) for h in range(hg)]
    loc = {}
    n_list = []
    for (h, ci) in units:
        sl = slice(ci * L, (ci + 1) * L)
        gcc = cols_c[ci][0:L, h:h + 1]
        gcr = rows_c[ci][h:h + 1, :]
        beta = cols_c[ci][L:2 * L, 4 + h:5 + h]
        g_last = gcc[L - 1:L, :]
        decay = jnp.exp(jnp.where(incl, gcc - gcr, NEG_BIG))
        kc = kn[h][sl]
        qc = qn[h][sl]
        kc_b = kc.astype(BF16)
        k_beta = kc * beta
        kk = _dot_nt(k_beta.astype(BF16), kc_b) * decay
        n_list.append(jnp.where(strict, kk, 0.0))
        rhs = jnp.concatenate([vh[h][sl] * beta, k_beta * jnp.exp(gcc)], axis=1)
        loc[(h, ci)] = dict(
            rhs=rhs.astype(BF16),
            qk=(_dot_nt(qc.astype(BF16), kc_b) * decay).astype(BF16),
            k_tail=(kc * jnp.exp(g_last - gcc)).astype(BF16),
            q_dec=(qc * jnp.exp(gcc)).astype(BF16),
            e_last=jnp.exp(g_last))
    t_invs = _unit_lower_inverses(n_list, r, c)
    for unit, t_inv in zip(units, t_invs):
        loc[unit]["sol"] = _dot(t_inv.astype(BF16), loc[unit]["rhs"])

    states = [s_ref[h] for h in range(hg)]
    outs = {}
    for ci in range(nchunk):
        sb = [st.astype(BF16) for st in states]
        v_new = []
        for h in range(hg):
            sol = loc[(h, ci)]["sol"]
            v_new.append((sol[:, 0:D] - _dot(sol[:, D:2 * D].astype(BF16), sb[h])).astype(BF16))
        for h in range(hg):
            u = loc[(h, ci)]
            outs[(h, ci)] = _dot(u["q_dec"], sb[h]) + _dot(u["qk"], v_new[h])
            states[h] = states[h] * u["e_last"] + _dot_tn(u["k_tail"], v_new[h])
    for h in range(hg):
        s_ref[h] = states[h]

    rows = []
    for ci in range(nchunk):
        heads_out = []
        for h in range(hg):
            o = outs[(h, ci)]
            ms = jnp.mean(o * o, axis=-1, keepdims=True)
            heads_out.append(o * lax.rsqrt(ms + EPS) * nw_ref[...])
        rows.append(jnp.concatenate(heads_out, axis=1))
    o_all = jnp.concatenate(rows, axis=0) if nchunk > 1 else rows[0]
    o_ref[...] = (o_all * _silu(z_ref[...].astype(F32))).astype(o_ref.dtype)


def _gdn(proj, gt, conv_w, dt_bias, a_log, norm_w, bsz, seq, heads, hg, q_blk, k_blk, v_blk,
         z_blk, gate_blk):
    d = HEAD_DIM
    width = heads * d
    gwid = hg * d
    ngrp = heads // hg
    tile = _pick_tile(seq, (512, 256, 128, 64))
    nt = seq // tile
    cwq, cwk, cwv = conv_w[:, :width], conv_w[:, width:2 * width], conv_w[:, 2 * width:]

    def grp_col(v):
        v = jnp.pad(v.astype(F32).reshape(ngrp, hg), ((0, 0), (0, GDN_GATE_ROWS - hg)))
        return v.reshape(ngrp, GDN_GATE_ROWS, 1)

    def tok(blk):
        return pl.BlockSpec((tile, gwid), lambda b, g, t: (b * nt + t, blk + g))

    def cw():
        return pl.BlockSpec((CONV_WIDTH, gwid), lambda b, g, t: (0, g))

    def gcol():
        return pl.BlockSpec((1, GDN_GATE_ROWS, 1), lambda b, g, t: (g, 0, 0))

    kern = functools.partial(_gdn_kernel, tile=tile, hg=hg, scale=d ** -0.5)
    return pl.pallas_call(
        kern,
        out_shape=jax.ShapeDtypeStruct((bsz * seq, width), BF16),
        grid=(bsz, ngrp, nt),
        in_specs=[tok(q_blk), tok(k_blk), tok(v_blk), tok(z_blk),
                  pl.BlockSpec((GDN_GATE_ROWS, tile), lambda b, g, t: (gate_blk + g, b * nt + t)),
                  cw(), cw(), cw(), gcol(), gcol(),
                  pl.BlockSpec((1, d), lambda b, g, t: (0, 0))],
        out_specs=pl.BlockSpec((tile, gwid), lambda b, g, t: (b * nt + t, g)),
        scratch_shapes=[pltpu.VMEM((tile + CONV_HALO, gwid), F32),
                        pltpu.VMEM((tile + CONV_HALO, gwid), F32),
                        pltpu.VMEM((tile + CONV_HALO, gwid), F32),
                        pltpu.VMEM((tile + 8, gwid), F32),
                        pltpu.VMEM((hg, d, d), F32)],
        compiler_params=pltpu.CompilerParams(
            dimension_semantics=("arbitrary", "arbitrary", "arbitrary"),
            vmem_limit_bytes=VMEM_LIMIT_V7X),
        name="gdn",
    )(proj, proj, proj, proj, gt, cwq, cwk, cwv, grp_col(dt_bias), grp_col(a_log),
      norm_w.reshape(1, d))


def _out_proj_kernel(x_ref, mf_ref, ms_ref, mg_ref, w_ref, o_ref):
    r1 = mf_ref.shape[1]
    r2 = r1 + ms_ref.shape[1]
    acc = _dot(mf_ref[...], w_ref[0, 0:r1, :])
    acc = acc + _dot(ms_ref[...], w_ref[0, r1:r2, :])
    acc = acc + _dot(mg_ref[...], w_ref[0, r2:, :])
    o_ref[...] = x_ref[...] + acc


def _out_proj(x2d, mf, ms, mg, w_all, layer):
    m, d = x2d.shape
    kdim = w_all.shape[1]
    tm = _pick_tile(m, (1024, 512, 256, 128))
    tn = _pick_tile(d, (1024, 512, 256, 128))

    def act(a):
        return pl.BlockSpec((tm, a.shape[1]), lambda i, j: (i, 0))

    return pl.pallas_call(
        _out_proj_kernel,
        out_shape=jax.ShapeDtypeStruct((m, d), F32),
        grid=(m // tm, d // tn),
        in_specs=[pl.BlockSpec((tm, tn), lambda i, j: (i, j)),
                  act(mf), act(ms), act(mg),
                  pl.BlockSpec((1, kdim, tn), lambda i, j: (layer, 0, j))],
        out_specs=pl.BlockSpec((tm, tn), lambda i, j: (i, j)),
        compiler_params=pltpu.CompilerParams(dimension_semantics=("arbitrary", "arbitrary"),
                                             vmem_limit_bytes=VMEM_LIMIT_V7X),
        name="out_proj",
    )(x2d, mf, ms, mg, w_all)


def _layer(x2d, bsz, seq, layer, norm_w, w_in, w_out_all, fox_b_f, fox_q_norm_w, fox_k_norm_w,
           fox_out_norm_w, ssd_conv_w, ssd_conv_b, ssd_dt_bias, ssd_a_log, ssd_d, ssd_norm_w,
           gdn_conv_w, gdn_dt_bias, gdn_a_log, gdn_norm_w):
    d_model = x2d.shape[1]
    fox_w = d_model // 4
    ssd_w = 3 * d_model // 8
    gdn_w = d_model - fox_w - ssd_w
    fox_h = fox_w // HEAD_DIM
    ssd_h = ssd_w // SSD_HEAD_DIM
    gdn_h = gdn_w // HEAD_DIM
    bcw = SSD_GROUPS * SSD_STATE
    hg = 4 if gdn_h % 4 == 0 else 2
    ngrp = gdn_h // hg
    assert fox_h <= FOX_GATE_ROWS and ssd_h <= SSD_GATE_ROWS and ssd_h % SSD_GROUPS == 0
    assert gdn_h % hg == 0 and seq % CHUNK == 0

    sizes = (3 * fox_w, fox_h, fox_w, ssd_w + 2 * bcw, ssd_w, ssd_h, 3 * gdn_w, gdn_w, gdn_h, gdn_h)
    offs = [0]
    for s in sizes:
        offs.append(offs[-1] + s)
    (o_fqkv, o_ff, o_fz, o_sxbc, o_sz, o_sdt, o_gqkv, o_gz, o_gb, o_ga) = offs[:-1]

    def cols(o, n):
        return w_in[:, o:o + n].astype(BF16)

    w_main = jnp.concatenate(
        [cols(o_sz, ssd_w), cols(o_sxbc, ssd_w + 2 * bcw), cols(o_fqkv, 3 * fox_w),
         cols(o_fz, fox_w), cols(o_gqkv, 3 * gdn_w), cols(o_gz, gdn_w)], axis=1)
    c_sz, c_sxs, c_sb = 0, ssd_w, 2 * ssd_w
    c_sc = c_sb + bcw
    c_fq = c_sc + bcw
    c_fz = c_fq + 3 * fox_w
    c_gq = c_fz + fox_w
    c_gz = c_gq + 3 * gdn_w
    gwid = hg * HEAD_DIM
    assert c_sb % bcw == 0 and c_fq % HEAD_DIM == 0 and c_gq % gwid == 0 and gdn_w % gwid == 0

    def gcols(o, n, total):
        return jnp.pad(w_in[:, o:o + n].astype(BF16), ((0, 0), (0, total - n)))

    gate_cols = [gcols(o_sdt, ssd_h, SSD_GATE_ROWS), gcols(o_ff, fox_h, FOX_GATE_ROWS)]
    for g in range(ngrp):
        gate_cols.append(gcols(o_ga + g * hg, hg, 4))
        gate_cols.append(gcols(o_gb + g * hg, hg, 4))
    n_gate_rows = SSD_GATE_ROWS + FOX_GATE_ROWS + GDN_GATE_ROWS * ngrp
    gate_cols.append(jnp.zeros((d_model, -n_gate_rows % 128), BF16))
    wg = jnp.concatenate(gate_cols, axis=1)
    fox_row_blk = SSD_GATE_ROWS // FOX_GATE_ROWS
    gdn_row_blk = (SSD_GATE_ROWS + FOX_GATE_ROWS) // GDN_GATE_ROWS

    h, gt = _prenorm(x2d, norm_w, wg, n_gate_rows)
    proj = _in_proj(h, w_main)

    b_f = jnp.pad(fox_b_f.astype(F32), (0, FOX_GATE_ROWS - fox_h)).reshape(FOX_GATE_ROWS, 1)
    f_cum = _fox_gate(gt, b_f, bsz, seq, fox_row_blk)
    f3 = f_cum.reshape(bsz * FOX_GATE_ROWS, 1, seq)
    mix_fox = _fox_attn(proj, f3, fox_q_norm_w, fox_k_norm_w, fox_out_norm_w, bsz, seq, fox_h,
                        c_fq // HEAD_DIM, (c_fq + fox_w) // HEAD_DIM,
                        (c_fq + 2 * fox_w) // HEAD_DIM, c_fz // HEAD_DIM)
    mix_ssd = _ssd(proj, gt, ssd_conv_w, ssd_conv_b, ssd_dt_bias, ssd_a_log, ssd_d, ssd_norm_w,
                   bsz, seq, ssd_h, c_sz // ssd_w, c_sxs // ssd_w, c_sb // bcw, c_sc // bcw)
    mix_gdn = _gdn(proj, gt, gdn_conv_w, gdn_dt_bias, gdn_a_log, gdn_norm_w, bsz, seq, gdn_h, hg,
                   c_gq // gwid, (c_gq + gdn_w) // gwid, (c_gq + 2 * gdn_w) // gwid,
                   c_gz // gwid, gdn_row_blk)

    return _out_proj(x2d, mix_fox, mix_ssd, mix_gdn, w_out_all, layer)


def kernel(x, norm_w, w_in, w_out, fox_b_f, fox_q_norm_w, fox_k_norm_w, fox_out_norm_w, ssd_conv_w, ssd_conv_b, ssd_dt_bias, ssd_A_log, ssd_D, ssd_norm_w, gdn_conv_w, gdn_dt_bias, gdn_A_log, gdn_norm_w):
    bsz, seq, d_model = x.shape
    x2d = x.reshape(bsz * seq, d_model)
    w_out_b = w_out.astype(BF16)
    for l in range(norm_w.shape[0]):
        x2d = _layer(x2d, bsz, seq, l, norm_w[l], w_in[l], w_out_b, fox_b_f[l], fox_q_norm_w[l],
                     fox_k_norm_w[l], fox_out_norm_w[l], ssd_conv_w[l], ssd_conv_b[l],
                     ssd_dt_bias[l], ssd_A_log[l], ssd_D[l], ssd_norm_w[l], gdn_conv_w[l],
                     gdn_dt_bias[l], gdn_A_log[l], gdn_norm_w[l])
    return x2d.reshape(bsz, seq, d_model)
```

```python
import functools

import jax
import jax.numpy as jnp
from jax import lax
from jax.experimental import pallas as pl
from jax.experimental.pallas import tpu as pltpu

F32 = jnp.float32
BF16 = jnp.bfloat16

EPS = 1e-6
CHUNK = 64
HEAD_DIM = 128
SSD_HEAD_DIM = 64
SSD_GROUPS = 4
SSD_STATE = 128
CONV_WIDTH = 4
CONV_HALO = 8
SSD_GATE_ROWS = 32
FOX_GATE_ROWS = 8
GDN_GATE_ROWS = 8
NEG_BIG = -1e30
VMEM_LIMIT_V7X = 56 * 1024 * 1024


def _dot(a, b):
    return jnp.dot(a, b, preferred_element_type=F32)


def _dot_nt(a, b):
    return lax.dot_general(a, b, (((1,), (1,)), ((), ())), preferred_element_type=F32)


def _dot_tn(a, b):
    return lax.dot_general(a, b, (((0,), (0,)), ((), ())), preferred_element_type=F32)


def _split3(a):
    a1 = a.astype(BF16)
    r1 = a - a1.astype(F32)
    a2 = r1.astype(BF16)
    r2 = r1 - a2.astype(F32)
    return a1, a2, r2.astype(BF16)


def _dot3(a, m):
    p1, p2, p3 = _split3(a)
    return _dot(p1, m) + _dot(p2, m) + _dot(p3, m)


def _dot3_nt(m, a):
    p1, p2, p3 = _split3(a)
    return _dot_nt(m, p1) + _dot_nt(m, p2) + _dot_nt(m, p3)


def _silu(x):
    return x * jax.nn.sigmoid(x)


def _softplus(x):
    return jnp.maximum(x, 0.0) + jnp.log1p(jnp.exp(-jnp.abs(x)))


def _log_sigmoid(x):
    return jnp.minimum(x, 0.0) - jnp.log1p(jnp.exp(-jnp.abs(x)))


def _iota2(shape):
    return (lax.broadcasted_iota(jnp.int32, shape, 0), lax.broadcasted_iota(jnp.int32, shape, 1))


def _pick_tile(n, candidates):
    for c in candidates:
        if n % c == 0 and n // c >= 2:
            return c
    return n


def _prenorm_kernel(x_ref, nw_ref, wg_ref, h_ref, gt_ref, wgt_ref):
    @pl.when(pl.program_id(0) == 0)
    def _():
        wgt_ref[...] = jnp.transpose(wg_ref[...].astype(F32)).astype(BF16)

    x = x_ref[...]
    ms = jnp.mean(x * x, axis=-1, keepdims=True)
    hb = (x * lax.rsqrt(ms + EPS) * nw_ref[...]).astype(BF16)
    h_ref[...] = hb
    gt_ref[...] = _dot_nt(wgt_ref[0:gt_ref.shape[0], :], hb)


def _prenorm(x2d, norm_w, wg, gate_rows):
    m, d = x2d.shape
    gcols = wg.shape[1]
    tm = _pick_tile(m, (512, 256, 128))
    return pl.pallas_call(
        _prenorm_kernel,
        out_shape=(jax.ShapeDtypeStruct((m, d), BF16), jax.ShapeDtypeStruct((gate_rows, m), F32)),
        grid=(m // tm,),
        in_specs=[pl.BlockSpec((tm, d), lambda i: (i, 0)),
                  pl.BlockSpec((1, d), lambda i: (0, 0)),
                  pl.BlockSpec((d, gcols), lambda i: (0, 0))],
        out_specs=(pl.BlockSpec((tm, d), lambda i: (i, 0)),
                   pl.BlockSpec((gate_rows, tm), lambda i: (0, i))),
        scratch_shapes=[pltpu.VMEM((gcols, d), BF16)],
        compiler_params=pltpu.CompilerParams(dimension_semantics=("arbitrary",),
                                             vmem_limit_bytes=VMEM_LIMIT_V7X),
        name="prenorm_gates",
    )(x2d, norm_w.reshape(1, d), wg)


def _in_proj_kernel(h_ref, w_ref, o_ref):
    o_ref[...] = _dot(h_ref[...], w_ref[...]).astype(o_ref.dtype)


def _in_proj(h, w):
    m, d = h.shape
    n = w.shape[1]
    tm = _pick_tile(m, (1024, 512, 256, 128))
    tn = _pick_tile(n, (1024, 512, 256, 128))
    return pl.pallas_call(
        _in_proj_kernel,
        out_shape=jax.ShapeDtypeStruct((m, n), BF16),
        grid=(m // tm, n // tn),
        in_specs=[pl.BlockSpec((tm, d), lambda i, j: (i, 0)),
                  pl.BlockSpec((d, tn), lambda i, j: (0, j))],
        out_specs=pl.BlockSpec((tm, tn), lambda i, j: (i, j)),
        compiler_params=pltpu.CompilerParams(dimension_semantics=("arbitrary", "arbitrary"),
                                             vmem_limit_bytes=VMEM_LIMIT_V7X),
        name="in_proj",
    )(h, w)


def _fox_gate_kernel(g_ref, b_ref, f_ref):
    seq = g_ref.shape[1]
    lf = _log_sigmoid(g_ref[...] + b_ref[...])
    r, c = _iota2((128, 128))
    upper = (r <= c).astype(BF16)
    carry = jnp.zeros((lf.shape[0], 1), F32)
    for i in range(seq // 128):
        cs = _dot3(lf[:, i * 128:(i + 1) * 128], upper) + carry
        f_ref[0, :, i * 128:(i + 1) * 128] = cs
        carry = cs[:, 127:128]


def _fox_gate(gt, b_f, bsz, seq, row_block):
    rows = FOX_GATE_ROWS
    return pl.pallas_call(
        _fox_gate_kernel,
        out_shape=jax.ShapeDtypeStruct((bsz, rows, seq), F32),
        grid=(bsz,),
        in_specs=[pl.BlockSpec((rows, seq), lambda b: (row_block, b)),
                  pl.BlockSpec((rows, 1), lambda b: (0, 0))],
        out_specs=pl.BlockSpec((1, rows, seq), lambda b: (b, 0, 0)),
        compiler_params=pltpu.CompilerParams(dimension_semantics=("arbitrary",)),
        name="fox_gate",
    )(gt, b_f)


LOG2E = 1.4426950408889634
FOX_VROWS = 144


def _fox_attn_kernel(q_ref, k_ref, v_ref, z_ref, f_ref, wq_ref, wk_ref, wo_ref, o_ref,
                     ka_ref, vat_ref, qa_ref, m_ref, acc_ref, s_ref, *, tk, scale):
    qi = pl.program_id(2)
    seq = k_ref.shape[0]
    d = HEAD_DIM
    tq = 2 * tk
    lane = lax.broadcasted_iota(jnp.int32, (tq, d), 1)

    @pl.when(qi == 0)
    def _():
        row8 = lax.broadcasted_iota(jnp.int32, (8, tk), 0)
        ones_row = (lax.broadcasted_iota(jnp.int32, (FOX_VROWS - d, tk), 0) == 0).astype(BF16)

        def body(i, carry):
            r0 = pl.multiple_of(i * tk, tk)
            kk = k_ref[pl.ds(r0, tk), :].astype(F32)
            ms = jnp.mean(kk * kk, axis=-1, keepdims=True)
            ka_ref[pl.ds(r0, tk), 0:d] = (kk * lax.rsqrt(ms + EPS) * wk_ref[...]).astype(BF16)
            frow = f_ref[0, :, pl.ds(r0, tk)]
            b1, b2, b3 = _split3((frow[:, 0:1] - frow) * LOG2E)
            parts = jnp.where(row8 == 0, b1.astype(F32),
                              jnp.where(row8 == 1, b2.astype(F32),
                                        jnp.where(row8 == 2, b3.astype(F32), 0.0)))
            parts = jnp.concatenate([parts, jnp.zeros((d - 8, tk), F32)], axis=0)
            ka_ref[pl.ds(r0, tk), d:2 * d] = jnp.transpose(parts).astype(BF16)
            vt = jnp.transpose(v_ref[pl.ds(r0, tk), :].astype(F32))
            vat_ref[i, 0:d, :] = vt.astype(BF16)
            vat_ref[i, d:FOX_VROWS, :] = ones_row
            return carry
        lax.fori_loop(0, seq // tk, body, 0)

    q = q_ref[...].astype(F32)
    ms = jnp.mean(q * q, axis=-1, keepdims=True)
    qa_ref[:, 0:d] = (q * lax.rsqrt(ms + EPS) * (wq_ref[...] * (scale * LOG2E))).astype(BF16)
    qa_ref[:, d:2 * d] = (lane < 3).astype(BF16)
    q0 = pl.multiple_of(qi * tq, tq)
    f_first = f_ref[0, :, pl.ds(q0, tk)][:, 0:1]

    m_ref[...] = jnp.full(m_ref.shape, NEG_BIG, F32)
    acc_ref[...] = jnp.zeros(acc_ref.shape, F32)

    def scores_to(slot, kb, halves=(0, 1)):
        r0 = pl.multiple_of(kb * tk, tk)
        lo, hi = halves[0] * tk, (halves[-1] + 1) * tk
        s_ref[slot, :, lo:hi] = _dot_nt(ka_ref[pl.ds(r0, tk), :], qa_ref[lo:hi, :])

    def consume(kb, slot, halves=(0, 1), causal_half=None):
        r0 = pl.multiple_of(kb * tk, tk)
        shift = (f_first - f_ref[0, :, pl.ds(r0, tk)][:, 0:1]) * LOG2E
        results = []
        for h in halves:
            cols = slice(h * tk, (h + 1) * tk)
            s = s_ref[slot, :, cols]
            if h == causal_half:
                r, c = _iota2((tk, tk))
                s = jnp.where(r <= c, s, NEG_BIG)
            m_prev = m_ref[:, cols]
            m_new = jnp.maximum(m_prev, jnp.max(s, axis=0, keepdims=True) + shift)
            p = jnp.exp2(s - (m_new - shift)).astype(BF16)
            alpha = jnp.exp2(m_prev - m_new)
            pv = _dot(vat_ref[kb], p)
            results.append((cols, m_new, alpha * acc_ref[:, cols] + pv))
        for cols, m_new, acc_new in results:
            m_ref[:, cols] = m_new
            acc_ref[:, cols] = acc_new

    scores_to(0, 0)

    def pair_body(j, carry):
        kb = 2 * j
        scores_to(1, kb + 1)
        consume(kb, 0)
        scores_to(0, kb + 2)
        consume(kb + 1, 1)
        return carry
    lax.fori_loop(0, qi, pair_body, 0)

    scores_to(1, 2 * qi + 1, halves=(1,))
    consume(2 * qi, 0, causal_half=0)
    consume(2 * qi + 1, 1, halves=(1,), causal_half=1)

    o = jnp.transpose(acc_ref[0:d, :] / acc_ref[d:d + 1, :])
    ms = jnp.mean(o * o, axis=-1, keepdims=True)
    o = o * lax.rsqrt(ms + EPS) * wo_ref[...]
    o_ref[...] = (o * _silu(z_ref[...].astype(F32))).astype(o_ref.dtype)


def _fox_attn(proj, f3, wq, wk, wo, bsz, seq, heads, q_blk, k_blk, v_blk, z_blk):
    d = HEAD_DIM
    tk = _pick_tile(seq // 2, (512, 256, 128))
    tq = 2 * tk
    nq = seq // tq
    assert seq % tq == 0
    kern = functools.partial(_fox_attn_kernel, tk=tk, scale=d ** -0.5)
    vec = pl.BlockSpec((1, d), lambda b, h, i: (0, 0))
    return pl.pallas_call(
        kern,
        out_shape=jax.ShapeDtypeStruct((bsz * seq, heads * d), BF16),
        grid=(bsz, heads, nq),
        in_specs=[pl.BlockSpec((tq, d), lambda b, h, i: (b * nq + i, q_blk + h)),
                  pl.BlockSpec((seq, d), lambda b, h, i: (b, k_blk + h)),
                  pl.BlockSpec((seq, d), lambda b, h, i: (b, v_blk + h)),
                  pl.BlockSpec((tq, d), lambda b, h, i: (b * nq + i, z_blk + h)),
                  pl.BlockSpec((1, 1, seq), lambda b, h, i: (b * FOX_GATE_ROWS + h, 0, 0)),
                  vec, vec, vec],
        out_specs=pl.BlockSpec((tq, d), lambda b, h, i: (b * nq + i, h)),
        scratch_shapes=[pltpu.VMEM((seq, 2 * d), BF16),
                        pltpu.VMEM((seq // tk, FOX_VROWS, tk), BF16),
                        pltpu.VMEM((tq, 2 * d), BF16),
                        pltpu.VMEM((1, tq), F32),
                        pltpu.VMEM((FOX_VROWS, tq), F32),
                        pltpu.VMEM((2, tk, tq), F32)],
        compiler_params=pltpu.CompilerParams(
            dimension_semantics=("arbitrary", "arbitrary", "arbitrary"),
            vmem_limit_bytes=VMEM_LIMIT_V7X),
        name="fox_attn",
    )(proj, proj, proj, proj, f3, wq.reshape(1, d), wk.reshape(1, d), wo.reshape(1, d))


def _conv_silu(buf, src_ref, w_ref, bias, tile):
    assert CONV_WIDTH == 4 and CONV_HALO == 8
    h0 = CONV_HALO
    buf[h0:h0 + tile, :] = src_ref[...].astype(F32)
    x = buf[...]
    z = pltpu.roll(x, 1, 0)
    u = w_ref[1:2, :] * x + w_ref[0:1, :] * z
    acc = w_ref[3:4, :] * x[h0:] + w_ref[2:3, :] * z[h0:] + pltpu.roll(u, 2, 0)[h0:]
    if bias is not None:
        acc = acc + bias
    buf[0:h0, :] = buf[tile:tile + h0, :]
    return _silu(acc)


def _ssd_kernel(z_ref, xs_ref, b_ref, c_ref, g_ref, cwx_ref, cwb_ref, cwc_ref,
                cbx_ref, cbb_ref, cbc_ref, dtb_ref, alog_ref, dexp_ref, nw_ref, e_ref,
                o_ref, bufx, bufb, bufc, ht_ref, *, tile, heads):
    L = CHUNK
    P = SSD_HEAD_DIM
    N = SSD_STATE
    per_group = heads // SSD_GROUPS
    gw = per_group * P
    nchunk = tile // L

    @pl.when(pl.program_id(1) == 0)
    def _():
        bufx[0:CONV_HALO, :] = jnp.zeros((CONV_HALO, bufx.shape[1]), F32)
        bufb[0:CONV_HALO, :] = jnp.zeros((CONV_HALO, bufb.shape[1]), F32)
        bufc[0:CONV_HALO, :] = jnp.zeros((CONV_HALO, bufc.shape[1]), F32)
        ht_ref[...] = jnp.zeros(ht_ref.shape, F32)

    xs = _conv_silu(bufx, xs_ref, cwx_ref, cbx_ref[...], tile)
    bm = _conv_silu(bufb, b_ref, cwb_ref, cbb_ref[...], tile)
    cm = _conv_silu(bufc, c_ref, cwc_ref, cbc_ref[...], tile)
    bm_b = bm.astype(BF16)
    cm_b = cm.astype(BF16)

    dt_row = _softplus(g_ref[...] + dtb_ref[...])
    a_row = dt_row * (-jnp.exp(alog_ref[...]))

    r, c = _iota2((L, L))
    tril = r >= c
    tri_u = (r <= c).astype(BF16)
    tri_l = tril.astype(BF16)
    eye = (r == c).astype(BF16)
    expand = e_ref[...]
    first_half = lax.broadcasted_iota(jnp.int32, (L, 2 * P), 1) < P

    chunks = []
    for ci in range(nchunk):
        sl = slice(ci * L, (ci + 1) * L)
        a_c = a_row[:, sl]
        cs_row = _dot3(a_c, tri_u)
        cs_col = _dot3_nt(tri_l, a_c)
        dt_col = _dot3_nt(eye, dt_row[:, sl])
        total = cs_col[L - 1:L, :]
        ds_col = jnp.exp(total - cs_col)
        ecs_col = jnp.exp(cs_col)
        cd_row = jnp.broadcast_to(jnp.exp(total), (8, total.shape[1]))
        stacked = jnp.concatenate([dt_col, ds_col, ecs_col, cd_row], axis=0)
        s1 = stacked.astype(BF16)
        s2 = (stacked - s1.astype(F32)).astype(BF16)
        ex = _dot(s1, expand) + _dot(s2, expand)
        xs_c = xs[sl]
        xdt = xs_c * ex[0:L]
        xdt_b = xdt.astype(BF16)
        cbs = [_dot_nt(cm_b[sl, g * N:(g + 1) * N], bm_b[sl, g * N:(g + 1) * N])
               for g in range(SSD_GROUPS)]
        y_pairs = []
        for hp in range(heads // 2):
            xpair = xdt_b[:, hp * 2 * P:(hp + 1) * 2 * P]
            acc = None
            for hd, keep in ((2 * hp, first_half), (2 * hp + 1, ~first_half)):
                seg = cs_col[:, hd:hd + 1] - cs_row[hd:hd + 1, :]
                lm = jnp.exp(jnp.where(tril, seg, NEG_BIG))
                part = _dot((cbs[hd // per_group] * lm).astype(BF16),
                            jnp.where(keep, xpair, jnp.zeros_like(xpair)))
                acc = part if acc is None else acc + part
            y_pairs.append(acc)
        chunks.append(dict(
            y_diag=jnp.concatenate(y_pairs, axis=1),
            xw=(xdt * ex[L:2 * L]).astype(BF16),
            ecs=ex[2 * L:3 * L], cd=ex[3 * L:3 * L + 1], xs=xs_c))

    hts = [ht_ref[g] for g in range(SSD_GROUPS)]
    rows_out = []
    for ci in range(nchunk):
        sl = slice(ci * L, (ci + 1) * L)
        ck = chunks[ci]
        y_off = []
        for g in range(SSD_GROUPS):
            cols = slice(g * gw, (g + 1) * gw)
            y_off.append(_dot(cm_b[sl, g * N:(g + 1) * N], hts[g].astype(BF16)) * ck["ecs"][:, cols])
            hts[g] = hts[g] * ck["cd"][:, cols] + _dot_tn(bm_b[sl, g * N:(g + 1) * N], ck["xw"][:, cols])
        y = ck["y_diag"] + jnp.concatenate(y_off, axis=1)
        y = (y + ck["xs"] * dexp_ref[...]) * _silu(z_ref[sl, :].astype(F32))
        outs = []
        for g in range(SSD_GROUPS):
            yg = y[:, g * gw:(g + 1) * gw]
            ms = jnp.mean(yg * yg, axis=-1, keepdims=True)
            outs.append(yg * lax.rsqrt(ms + EPS))
        rows_out.append(jnp.concatenate(outs, axis=1) * nw_ref[...])
    for g in range(SSD_GROUPS):
        ht_ref[g] = hts[g]
    o_all = jnp.concatenate(rows_out, axis=0) if nchunk > 1 else rows_out[0]
    o_ref[...] = o_all.astype(o_ref.dtype)


def _ssd(proj, gt, conv_w, conv_b, dt_bias, a_log, d_skip, norm_w, bsz, seq, heads,
         z_blk, xs_blk, b_blk, c_blk):
    width = heads * SSD_HEAD_DIM
    bcw = SSD_GROUPS * SSD_STATE
    tile = _pick_tile(seq, (512, 256, 128, 64))
    nt = seq // tile
    rows = SSD_GATE_ROWS
    pad = rows - heads

    def col(v):
        return jnp.pad(v.astype(F32), (0, pad)).reshape(rows, 1)

    head_of_col = jnp.arange(width) // SSD_HEAD_DIM
    expand = (jnp.arange(rows)[:, None] == head_of_col[None, :]).astype(BF16)
    d_exp = jnp.repeat(d_skip.astype(F32), SSD_HEAD_DIM).reshape(1, width)
    cwx, cwb, cwc = conv_w[:, :width], conv_w[:, width:width + bcw], conv_w[:, width + bcw:]
    cbx = conv_b[:width].reshape(1, width)
    cbb = conv_b[width:width + bcw].reshape(1, bcw)
    cbc = conv_b[width + bcw:].reshape(1, bcw)

    def full(shape):
        return pl.BlockSpec(shape, lambda b, t: (0,) * len(shape))

    kern = functools.partial(_ssd_kernel, tile=tile, heads=heads)
    return pl.pallas_call(
        kern,
        out_shape=jax.ShapeDtypeStruct((bsz * seq, width), BF16),
        grid=(bsz, nt),
        in_specs=[pl.BlockSpec((tile, width), lambda b, t: (b * nt + t, z_blk)),
                  pl.BlockSpec((tile, width), lambda b, t: (b * nt + t, xs_blk)),
                  pl.BlockSpec((tile, bcw), lambda b, t: (b * nt + t, b_blk)),
                  pl.BlockSpec((tile, bcw), lambda b, t: (b * nt + t, c_blk)),
                  pl.BlockSpec((rows, tile), lambda b, t: (0, b * nt + t)),
                  full((CONV_WIDTH, width)), full((CONV_WIDTH, bcw)), full((CONV_WIDTH, bcw)),
                  full((1, width)), full((1, bcw)), full((1, bcw)),
                  full((rows, 1)), full((rows, 1)), full((1, width)), full((1, width)),
                  full((rows, width))],
        out_specs=pl.BlockSpec((tile, width), lambda b, t: (b * nt + t, 0)),
        scratch_shapes=[pltpu.VMEM((tile + CONV_HALO, width), F32),
                        pltpu.VMEM((tile + CONV_HALO, bcw), F32),
                        pltpu.VMEM((tile + CONV_HALO, bcw), F32),
                        pltpu.VMEM((SSD_GROUPS, SSD_STATE, width // SSD_GROUPS), F32)],
        compiler_params=pltpu.CompilerParams(dimension_semantics=("arbitrary", "arbitrary"),
                                             vmem_limit_bytes=VMEM_LIMIT_V7X),
        name="ssd",
    )(proj, proj, proj, proj, gt, cwx, cwb, cwc, cbx, cbb, cbc, col(dt_bias), col(a_log),
      d_exp, norm_w.reshape(1, width), expand)


def _unit_lower_inverses(n_list, r, c):
    eye = (r == c).astype(F32)
    same2 = (r >> 1) == (c >> 1)
    ts = [eye - jnp.where(same2, n, 0.0) for n in n_list]
    s = 2
    while s < CHUNK:
        sh = s.bit_length()
        in_block = (r >> sh) == (c >> sh)
        low_left = in_block & ((r & (2 * s - 1)) >= s) & ((c & (2 * s - 1)) < s)
        tbs = [t.astype(BF16) for t in ts]
        xs = [_dot(jnp.where(low_left, n, 0.0).astype(BF16), tb) for n, tb in zip(n_list, tbs)]
        ts = [t - _dot(tb, x.astype(BF16)) for t, tb, x in zip(ts, tbs, xs)]
        s *= 2
    return ts


def _gdn_kernel(q_ref, k_ref, v_ref, z_ref, g_ref, cwq_ref, cwk_ref, cwv_ref, dtb_ref, alog_ref,
                nw_ref, o_ref, bufq, bufk, bufv, s_ref, *, tile, hg, scale):
    L = CHUNK
    D = HEAD_DIM

    @pl.when(pl.program_id(2) == 0)
    def _():
        bufq[0:CONV_HALO, :] = jnp.zeros((CONV_HALO, bufq.shape[1]), F32)
        bufk[0:CONV_HALO, :] = jnp.zeros((CONV_HALO, bufk.shape[1]), F32)
        bufv[0:CONV_HALO, :] = jnp.zeros((CONV_HALO, bufv.shape[1]), F32)
        s_ref[...] = jnp.zeros(s_ref.shape, F32)

    q_all = _conv_silu(bufq, q_ref, cwq_ref, None, tile)
    k_all = _conv_silu(bufk, k_ref, cwk_ref, None, tile)
    v_all = _conv_silu(bufv, v_ref, cwv_ref, None, tile)

    gl = g_ref[...]
    rows8 = lax.broadcasted_iota(jnp.int32, gl.shape, 0)
    decay_log = -jnp.exp(alog_ref[0]) * _softplus(gl + dtb_ref[0])
    gates = jnp.where(rows8 < 4, decay_log, jax.nn.sigmoid(gl))
    gates = jnp.concatenate([gates, jnp.zeros_like(gates)], axis=0)

    r, c = _iota2((L, L))
    incl = r >= c
    strict = r > c
    tri_u = (r <= c).astype(BF16)
    tri_eye = jnp.concatenate([incl.astype(BF16), (r == c).astype(BF16)], axis=0)

    nchunk = tile // L
    cols_c = [_dot3_nt(tri_eye, gates[:, ci * L:(ci + 1) * L]) for ci in range(nchunk)]
    rows_c = [_dot3(gates[:, ci * L:(ci + 1) * L], tri_u) for ci in range(nchunk)]

    qn, kn, vh = [], [], []
    for h in range(hg):
        hs = slice(h * D, (h + 1) * D)
        qh = q_all[:, hs]
        kh = k_all[:, hs]
        qn.append(qh * (lax.rsqrt(jnp.sum(qh * qh, axis=-1, keepdims=True) + EPS) * scale))
        kn.append(kh * lax.rsqrt(jnp.sum(kh * kh, axis=-1, keepdims=True) + EPS))
        vh.append(v_all[:, hs])

    units = [(h, ci) for ci in range(nchunk) for h in range(hg)]
    loc = {}
    n_list = []
    for (h, ci) in units:
        sl = slice(ci * L, (ci + 1) * L)
        gcc = cols_c[ci][0:L, h:h + 1]
        gcr = rows_c[ci][h:h + 1, :]
        beta = cols_c[ci][L:2 * L, 4 + h:5 + h]
        g_last = gcc[L - 1:L, :]
        decay = jnp.exp(jnp.where(incl, gcc - gcr, NEG_BIG))
        kc = kn[h][sl]
        qc = qn[h][sl]
        kc_b = kc.astype(BF16)
        k_beta = kc * beta
        kk = _dot_nt(k_beta.astype(BF16), kc_b) * decay
        n_list.append(jnp.where(strict, kk, 0.0))
        rhs = jnp.concatenate([vh[h][sl] * beta, k_beta * jnp.exp(gcc)], axis=1)
        loc[(h, ci)] = dict(
            rhs=rhs.astype(BF16),
            qk=(_dot_nt(qc.astype(BF16), kc_b) * decay).astype(BF16),
            k_tail=(kc * jnp.exp(g_last - gcc)).astype(BF16),
            q_dec=(qc * jnp.exp(gcc)).astype(BF16),
            e_last=jnp.exp(g_last))
    t_invs = _unit_lower_inverses(n_list, r, c)
    for unit, t_inv in zip(units, t_invs):
        loc[unit]["sol"] = _dot(t_inv.astype(BF16), loc[unit]["rhs"])

    states = [s_ref[h] for h in range(hg)]
    outs = {}
    for ci in range(nchunk):
        sb = [st.astype(BF16) for st in states]
        v_new = []
        for h in range(hg):
            sol = loc[(h, ci)]["sol"]
            v_new.append((sol[:, 0:D] - _dot(sol[:, D:2 * D].astype(BF16), sb[h])).astype(BF16))
        for h in range(hg):
            u = loc[(h, ci)]
            outs[(h, ci)] = _dot(u["q_dec"], sb[h]) + _dot(u["qk"], v_new[h])
            states[h] = states[h] * u["e_last"] + _dot_tn(u["k_tail"], v_new[h])
    for h in range(hg):
        s_ref[h] = states[h]

    rows = []
    for ci in range(nchunk):
        heads_out = []
        for h in range(hg):
            o = outs[(h, ci)]
            ms = jnp.mean(o * o, axis=-1, keepdims=True)
            heads_out.append(o * lax.rsqrt(ms + EPS) * nw_ref[...])
        rows.append(jnp.concatenate(heads_out, axis=1))
    o_all = jnp.concatenate(rows, axis=0) if nchunk > 1 else rows[0]
    o_ref[...] = (o_all * _silu(z_ref[...].astype(F32))).astype(o_ref.dtype)


def _gdn(proj, gt, conv_w, dt_bias, a_log, norm_w, bsz, seq, heads, hg, q_blk, k_blk, v_blk,
         z_blk, gate_blk):
    d = HEAD_DIM
    width = heads * d
    gwid = hg * d
    ngrp = heads // hg
    tile = _pick_tile(seq, (512, 256, 128, 64))
    nt = seq // tile
    cwq, cwk, cwv = conv_w[:, :width], conv_w[:, width:2 * width], conv_w[:, 2 * width:]

    def grp_col(v):
        v = jnp.pad(v.astype(F32).reshape(ngrp, hg), ((0, 0), (0, GDN_GATE_ROWS - hg)))
        return v.reshape(ngrp, GDN_GATE_ROWS, 1)

    def tok(blk):
        return pl.BlockSpec((tile, gwid), lambda b, g, t: (b * nt + t, blk + g))

    def cw():
        return pl.BlockSpec((CONV_WIDTH, gwid), lambda b, g, t: (0, g))

    def gcol():
        return pl.BlockSpec((1, GDN_GATE_ROWS, 1), lambda b, g, t: (g, 0, 0))

    kern = functools.partial(_gdn_kernel, tile=tile, hg=hg, scale=d ** -0.5)
    return pl.pallas_call(
        kern,
        out_shape=jax.ShapeDtypeStruct((bsz * seq, width), BF16),
        grid=(bsz, ngrp, nt),
        in_specs=[tok(q_blk), tok(k_blk), tok(v_blk), tok(z_blk),
                  pl.BlockSpec((GDN_GATE_ROWS, tile), lambda b, g, t: (gate_blk + g, b * nt + t)),
                  cw(), cw(), cw(), gcol(), gcol(),
                  pl.BlockSpec((1, d), lambda b, g, t: (0, 0))],
        out_specs=pl.BlockSpec((tile, gwid), lambda b, g, t: (b * nt + t, g)),
        scratch_shapes=[pltpu.VMEM((tile + CONV_HALO, gwid), F32),
                        pltpu.VMEM((tile + CONV_HALO, gwid), F32),
                        pltpu.VMEM((tile + CONV_HALO, gwid), F32),
                        pltpu.VMEM((hg, d, d), F32)],
        compiler_params=pltpu.CompilerParams(
            dimension_semantics=("arbitrary", "arbitrary", "arbitrary"),
            vmem_limit_bytes=VMEM_LIMIT_V7X),
        name="gdn",
    )(proj, proj, proj, proj, gt, cwq, cwk, cwv, grp_col(dt_bias), grp_col(a_log),
      norm_w.reshape(1, d))


def _out_proj_kernel(x_ref, mf_ref, ms_ref, mg_ref, w_ref, o_ref):
    r1 = mf_ref.shape[1]
    r2 = r1 + ms_ref.shape[1]
    acc = _dot(mf_ref[...], w_ref[0, 0:r1, :])
    acc = acc + _dot(ms_ref[...], w_ref[0, r1:r2, :])
    acc = acc + _dot(mg_ref[...], w_ref[0, r2:, :])
    o_ref[...] = x_ref[...] + acc


def _out_proj(x2d, mf, ms, mg, w_all, layer):
    m, d = x2d.shape
    kdim = w_all.shape[1]
    tm = _pick_tile(m, (1024, 512, 256, 128))
    tn = _pick_tile(d, (1024, 512, 256, 128))

    def act(a):
        return pl.BlockSpec((tm, a.shape[1]), lambda i, j: (i, 0))

    return pl.pallas_call(
        _out_proj_kernel,
        out_shape=jax.ShapeDtypeStruct((m, d), F32),
        grid=(m // tm, d // tn),
        in_specs=[pl.BlockSpec((tm, tn), lambda i, j: (i, j)),
                  act(mf), act(ms), act(mg),
                  pl.BlockSpec((1, kdim, tn), lambda i, j: (layer, 0, j))],
        out_specs=pl.BlockSpec((tm, tn), lambda i, j: (i, j)),
        compiler_params=pltpu.CompilerParams(dimension_semantics=("arbitrary", "arbitrary"),
                                             vmem_limit_bytes=VMEM_LIMIT_V7X),
        name="out_proj",
    )(x2d, mf, ms, mg, w_all)


def _layer(x2d, bsz, seq, layer, norm_w, w_in, w_out_all, fox_b_f, fox_q_norm_w, fox_k_norm_w,
           fox_out_norm_w, ssd_conv_w, ssd_conv_b, ssd_dt_bias, ssd_a_log, ssd_d, ssd_norm_w,
           gdn_conv_w, gdn_dt_bias, gdn_a_log, gdn_norm_w):
    d_model = x2d.shape[1]
    fox_w = d_model // 4
    ssd_w = 3 * d_model // 8
    gdn_w = d_model - fox_w - ssd_w
    fox_h = fox_w // HEAD_DIM
    ssd_h = ssd_w // SSD_HEAD_DIM
    gdn_h = gdn_w // HEAD_DIM
    bcw = SSD_GROUPS * SSD_STATE
    hg = 4 if gdn_h % 4 == 0 else 2
    ngrp = gdn_h // hg
    assert fox_h <= FOX_GATE_ROWS and ssd_h <= SSD_GATE_ROWS and ssd_h % SSD_GROUPS == 0
    assert gdn_h % hg == 0 and seq % CHUNK == 0

    sizes = (3 * fox_w, fox_h, fox_w, ssd_w + 2 * bcw, ssd_w, ssd_h, 3 * gdn_w, gdn_w, gdn_h, gdn_h)
    offs = [0]
    for s in sizes:
        offs.append(offs[-1] + s)
    (o_fqkv, o_ff, o_fz, o_sxbc, o_sz, o_sdt, o_gqkv, o_gz, o_gb, o_ga) = offs[:-1]

    def cols(o, n):
        return w_in[:, o:o + n].astype(BF16)

    w_main = jnp.concatenate(
        [cols(o_sz, ssd_w), cols(o_sxbc, ssd_w + 2 * bcw), cols(o_fqkv, 3 * fox_w),
         cols(o_fz, fox_w), cols(o_gqkv, 3 * gdn_w), cols(o_gz, gdn_w)], axis=1)
    c_sz, c_sxs, c_sb = 0, ssd_w, 2 * ssd_w
    c_sc = c_sb + bcw
    c_fq = c_sc + bcw
    c_fz = c_fq + 3 * fox_w
    c_gq = c_fz + fox_w
    c_gz = c_gq + 3 * gdn_w
    gwid = hg * HEAD_DIM
    assert c_sb % bcw == 0 and c_fq % HEAD_DIM == 0 and c_gq % gwid == 0 and gdn_w % gwid == 0

    def gcols(o, n, total):
        return jnp.pad(w_in[:, o:o + n].astype(BF16), ((0, 0), (0, total - n)))

    gate_cols = [gcols(o_sdt, ssd_h, SSD_GATE_ROWS), gcols(o_ff, fox_h, FOX_GATE_ROWS)]
    for g in range(ngrp):
        gate_cols.append(gcols(o_ga + g * hg, hg, 4))
        gate_cols.append(gcols(o_gb + g * hg, hg, 4))
    n_gate_rows = SSD_GATE_ROWS + FOX_GATE_ROWS + GDN_GATE_ROWS * ngrp
    gate_cols.append(jnp.zeros((d_model, -n_gate_rows % 128), BF16))
    wg = jnp.concatenate(gate_cols, axis=1)
    fox_row_blk = SSD_GATE_ROWS // FOX_GATE_ROWS
    gdn_row_blk = (SSD_GATE_ROWS + FOX_GATE_ROWS) // GDN_GATE_ROWS

    h, gt = _prenorm(x2d, norm_w, wg, n_gate_rows)
    proj = _in_proj(h, w_main)

    b_f = jnp.pad(fox_b_f.astype(F32), (0, FOX_GATE_ROWS - fox_h)).reshape(FOX_GATE_ROWS, 1)
    f_cum = _fox_gate(gt, b_f, bsz, seq, fox_row_blk)
    f3 = f_cum.reshape(bsz * FOX_GATE_ROWS, 1, seq)
    mix_fox = _fox_attn(proj, f3, fox_q_norm_w, fox_k_norm_w, fox_out_norm_w, bsz, seq, fox_h,
                        c_fq // HEAD_DIM, (c_fq + fox_w) // HEAD_DIM,
                        (c_fq + 2 * fox_w) // HEAD_DIM, c_fz // HEAD_DIM)
    mix_ssd = _ssd(proj, gt, ssd_conv_w, ssd_conv_b, ssd_dt_bias, ssd_a_log, ssd_d, ssd_norm_w,
                   bsz, seq, ssd_h, c_sz // ssd_w, c_sxs // ssd_w, c_sb // bcw, c_sc // bcw)
    mix_gdn = _gdn(proj, gt, gdn_conv_w, gdn_dt_bias, gdn_a_log, gdn_norm_w, bsz, seq, gdn_h, hg,
                   c_gq // gwid, (c_gq + gdn_w) // gwid, (c_gq + 2 * gdn_w) // gwid,
                   c_gz // gwid, gdn_row_blk)

    return _out_proj(x2d, mix_fox, mix_ssd, mix_gdn, w_out_all, layer)


def kernel(x, norm_w, w_in, w_out, fox_b_f, fox_q_norm_w, fox_k_norm_w, fox_out_norm_w, ssd_conv_w, ssd_conv_b, ssd_dt_bias, ssd_A_log, ssd_D, ssd_norm_w, gdn_conv_w, gdn_dt_bias, gdn_A_log, gdn_norm_w):
    bsz, seq, d_model = x.shape
    x2d = x.reshape(bsz * seq, d_model)
    w_out_b = w_out.astype(BF16)
    for l in range(norm_w.shape[0]):
        x2d = _layer(x2d, bsz, seq, l, norm_w[l], w_in[l], w_out_b, fox_b_f[l], fox_q_norm_w[l],
                     fox_k_norm_w[l], fox_out_norm_w[l], ssd_conv_w[l], ssd_conv_b[l],
                     ssd_dt_bias[l], ssd_A_log[l], ssd_D[l], ssd_norm_w[l], gdn_conv_w[l],
                     gdn_dt_bias[l], gdn_A_log[l], gdn_norm_w[l])
    return x2d.reshape(bsz, seq, d_model)
```

```python
import functools

import jax
import jax.numpy as jnp
from jax import lax
from jax.experimental import pallas as pl
from jax.experimental.pallas import tpu as pltpu

F32 = jnp.float32
BF16 = jnp.bfloat16

EPS = 1e-6
CHUNK = 64
HEAD_DIM = 128
SSD_HEAD_DIM = 64
SSD_GROUPS = 4
SSD_STATE = 128
CONV_WIDTH = 4
CONV_HALO = 8
SSD_GATE_ROWS = 32
FOX_GATE_ROWS = 8
GDN_GATE_ROWS = 8
NEG_BIG = -1e30
VMEM_LIMIT_V7X = 56 * 1024 * 1024


def _dot(a, b):
    return jnp.dot(a, b, preferred_element_type=F32)


def _dot_nt(a, b):
    return lax.dot_general(a, b, (((1,), (1,)), ((), ())), preferred_element_type=F32)


def _dot_tn(a, b):
    return lax.dot_general(a, b, (((0,), (0,)), ((), ())), preferred_element_type=F32)


def _split3(a):
    a1 = a.astype(BF16)
    r1 = a - a1.astype(F32)
    a2 = r1.astype(BF16)
    r2 = r1 - a2.astype(F32)
    return a1, a2, r2.astype(BF16)


def _dot3(a, m):
    p1, p2, p3 = _split3(a)
    return _dot(p1, m) + _dot(p2, m) + _dot(p3, m)


def _dot3_nt(m, a):
    p1, p2, p3 = _split3(a)
    return _dot_nt(m, p1) + _dot_nt(m, p2) + _dot_nt(m, p3)


def _silu(x):
    return x * jax.nn.sigmoid(x)


def _softplus(x):
    return jnp.maximum(x, 0.0) + jnp.log1p(jnp.exp(-jnp.abs(x)))


def _log_sigmoid(x):
    return jnp.minimum(x, 0.0) - jnp.log1p(jnp.exp(-jnp.abs(x)))


def _iota2(shape):
    return (lax.broadcasted_iota(jnp.int32, shape, 0), lax.broadcasted_iota(jnp.int32, shape, 1))


def _pick_tile(n, candidates):
    for c in candidates:
        if n % c == 0 and n // c >= 2:
            return c
    return n


def _prenorm_kernel(x_ref, nw_ref, wg_ref, h_ref, gt_ref, wgt_ref):
    @pl.when(pl.program_id(0) == 0)
    def _():
        wgt_ref[...] = jnp.transpose(wg_ref[...].astype(F32)).astype(BF16)

    x = x_ref[...]
    ms = jnp.mean(x * x, axis=-1, keepdims=True)
    hb = (x * lax.rsqrt(ms + EPS) * nw_ref[...]).astype(BF16)
    h_ref[...] = hb
    gt_ref[...] = _dot_nt(wgt_ref[0:gt_ref.shape[0], :], hb)


def _prenorm(x2d, norm_w, wg, gate_rows):
    m, d = x2d.shape
    gcols = wg.shape[1]
    tm = _pick_tile(m, (512, 256, 128))
    return pl.pallas_call(
        _prenorm_kernel,
        out_shape=(jax.ShapeDtypeStruct((m, d), BF16), jax.ShapeDtypeStruct((gate_rows, m), F32)),
        grid=(m // tm,),
        in_specs=[pl.BlockSpec((tm, d), lambda i: (i, 0)),
                  pl.BlockSpec((1, d), lambda i: (0, 0)),
                  pl.BlockSpec((d, gcols), lambda i: (0, 0))],
        out_specs=(pl.BlockSpec((tm, d), lambda i: (i, 0)),
                   pl.BlockSpec((gate_rows, tm), lambda i: (0, i))),
        scratch_shapes=[pltpu.VMEM((gcols, d), BF16)],
        compiler_params=pltpu.CompilerParams(dimension_semantics=("arbitrary",),
                                             vmem_limit_bytes=VMEM_LIMIT_V7X),
        name="prenorm_gates",
    )(x2d, norm_w.reshape(1, d), wg)


def _in_proj_kernel(h_ref, w_ref, o_ref):
    o_ref[...] = _dot(h_ref[...], w_ref[...]).astype(o_ref.dtype)


def _in_proj(h, w):
    m, d = h.shape
    n = w.shape[1]
    tm = _pick_tile(m, (1024, 512, 256, 128))
    tn = _pick_tile(n, (1024, 512, 256, 128))
    return pl.pallas_call(
        _in_proj_kernel,
        out_shape=jax.ShapeDtypeStruct((m, n), BF16),
        grid=(m // tm, n // tn),
        in_specs=[pl.BlockSpec((tm, d), lambda i, j: (i, 0)),
                  pl.BlockSpec((d, tn), lambda i, j: (0, j))],
        out_specs=pl.BlockSpec((tm, tn), lambda i, j: (i, j)),
        compiler_params=pltpu.CompilerParams(dimension_semantics=("arbitrary", "arbitrary"),
                                             vmem_limit_bytes=VMEM_LIMIT_V7X),
        name="in_proj",
    )(h, w)


def _fox_gate_kernel(g_ref, b_ref, f_ref):
    seq = g_ref.shape[1]
    lf = _log_sigmoid(g_ref[...] + b_ref[...])
    r, c = _iota2((128, 128))
    upper = (r <= c).astype(BF16)
    carry = jnp.zeros((lf.shape[0], 1), F32)
    for i in range(seq // 128):
        cs = _dot3(lf[:, i * 128:(i + 1) * 128], upper) + carry
        f_ref[0, :, i * 128:(i + 1) * 128] = cs
        carry = cs[:, 127:128]


def _fox_gate(gt, b_f, bsz, seq, row_block):
    rows = FOX_GATE_ROWS
    return pl.pallas_call(
        _fox_gate_kernel,
        out_shape=jax.ShapeDtypeStruct((bsz, rows, seq), F32),
        grid=(bsz,),
        in_specs=[pl.BlockSpec((rows, seq), lambda b: (row_block, b)),
                  pl.BlockSpec((rows, 1), lambda b: (0, 0))],
        out_specs=pl.BlockSpec((1, rows, seq), lambda b: (b, 0, 0)),
        compiler_params=pltpu.CompilerParams(dimension_semantics=("arbitrary",)),
        name="fox_gate",
    )(gt, b_f)


LOG2E = 1.4426950408889634
FOX_VROWS = 144


def _fox_attn_kernel(q_ref, k_ref, v_ref, z_ref, f_ref, wq_ref, wk_ref, wo_ref, o_ref,
                     ka_ref, vat_ref, qa_ref, m_ref, acc_ref, s_ref, *, tk, scale):
    qi = pl.program_id(2)
    seq = k_ref.shape[0]
    d = HEAD_DIM
    tq = 2 * tk
    lane = lax.broadcasted_iota(jnp.int32, (tq, d), 1)

    @pl.when(qi == 0)
    def _():
        row8 = lax.broadcasted_iota(jnp.int32, (8, tk), 0)
        ones_row = (lax.broadcasted_iota(jnp.int32, (FOX_VROWS - d, tk), 0) == 0).astype(BF16)

        def body(i, carry):
            r0 = pl.multiple_of(i * tk, tk)
            kk = k_ref[pl.ds(r0, tk), :].astype(F32)
            ms = jnp.mean(kk * kk, axis=-1, keepdims=True)
            ka_ref[pl.ds(r0, tk), 0:d] = (kk * lax.rsqrt(ms + EPS) * wk_ref[...]).astype(BF16)
            frow = f_ref[0, :, pl.ds(r0, tk)]
            b1, b2, b3 = _split3((frow[:, 0:1] - frow) * LOG2E)
            parts = jnp.where(row8 == 0, b1.astype(F32),
                              jnp.where(row8 == 1, b2.astype(F32),
                                        jnp.where(row8 == 2, b3.astype(F32), 0.0)))
            parts = jnp.concatenate([parts, jnp.zeros((d - 8, tk), F32)], axis=0)
            ka_ref[pl.ds(r0, tk), d:2 * d] = jnp.transpose(parts).astype(BF16)
            vt = jnp.transpose(v_ref[pl.ds(r0, tk), :].astype(F32))
            vat_ref[i, 0:d, :] = vt.astype(BF16)
            vat_ref[i, d:FOX_VROWS, :] = ones_row
            return carry
        lax.fori_loop(0, seq // tk, body, 0)

    q = q_ref[...].astype(F32)
    ms = jnp.mean(q * q, axis=-1, keepdims=True)
    qa_ref[:, 0:d] = (q * lax.rsqrt(ms + EPS) * (wq_ref[...] * (scale * LOG2E))).astype(BF16)
    qa_ref[:, d:2 * d] = (lane < 3).astype(BF16)
    q0 = pl.multiple_of(qi * tq, tq)
    f_first = f_ref[0, :, pl.ds(q0, tk)][:, 0:1]

    m_ref[...] = jnp.full(m_ref.shape, NEG_BIG, F32)
    acc_ref[...] = jnp.zeros(acc_ref.shape, F32)

    def scores_to(slot, kb, halves=(0, 1)):
        r0 = pl.multiple_of(kb * tk, tk)
        lo, hi = halves[0] * tk, (halves[-1] + 1) * tk
        s_ref[slot, :, lo:hi] = _dot_nt(ka_ref[pl.ds(r0, tk), :], qa_ref[lo:hi, :])

    def consume(kb, slot, halves=(0, 1), causal_half=None):
        r0 = pl.multiple_of(kb * tk, tk)
        shift = (f_first - f_ref[0, :, pl.ds(r0, tk)][:, 0:1]) * LOG2E
        results = []
        for h in halves:
            cols = slice(h * tk, (h + 1) * tk)
            s = s_ref[slot, :, cols]
            if h == causal_half:
                r, c = _iota2((tk, tk))
                s = jnp.where(r <= c, s, NEG_BIG)
            m_prev = m_ref[:, cols]
            m_new = jnp.maximum(m_prev, jnp.max(s, axis=0, keepdims=True) + shift)
            p = jnp.exp2(s - (m_new - shift)).astype(BF16)
            alpha = jnp.exp2(m_prev - m_new)
            pv = _dot(vat_ref[kb], p)
            results.append((cols, m_new, alpha * acc_ref[:, cols] + pv))
        for cols, m_new, acc_new in results:
            m_ref[:, cols] = m_new
            acc_ref[:, cols] = acc_new

    scores_to(0, 0)

    def pair_body(j, carry):
        kb = 2 * j
        scores_to(1, kb + 1)
        consume(kb, 0)
        scores_to(0, kb + 2)
        consume(kb + 1, 1)
        return carry
    lax.fori_loop(0, qi, pair_body, 0)

    scores_to(1, 2 * qi + 1, halves=(1,))
    consume(2 * qi, 0, causal_half=0)
    consume(2 * qi + 1, 1, halves=(1,), causal_half=1)

    o = jnp.transpose(acc_ref[0:d, :] / acc_ref[d:d + 1, :])
    ms = jnp.mean(o * o, axis=-1, keepdims=True)
    o = o * lax.rsqrt(ms + EPS) * wo_ref[...]
    o_ref[...] = (o * _silu(z_ref[...].astype(F32))).astype(o_ref.dtype)


def _fox_attn(proj, f3, wq, wk, wo, bsz, seq, heads, q_blk, k_blk, v_blk, z_blk):
    d = HEAD_DIM
    tk = _pick_tile(seq // 2, (512, 256, 128))
    tq = 2 * tk
    nq = seq // tq
    assert seq % tq == 0
    kern = functools.partial(_fox_attn_kernel, tk=tk, scale=d ** -0.5)
    vec = pl.BlockSpec((1, d), lambda b, h, i: (0, 0))
    return pl.pallas_call(
        kern,
        out_shape=jax.ShapeDtypeStruct((bsz * seq, heads * d), BF16),
        grid=(bsz, heads, nq),
        in_specs=[pl.BlockSpec((tq, d), lambda b, h, i: (b * nq + i, q_blk + h)),
                  pl.BlockSpec((seq, d), lambda b, h, i: (b, k_blk + h)),
                  pl.BlockSpec((seq, d), lambda b, h, i: (b, v_blk + h)),
                  pl.BlockSpec((tq, d), lambda b, h, i: (b * nq + i, z_blk + h)),
                  pl.BlockSpec((1, 1, seq), lambda b, h, i: (b * FOX_GATE_ROWS + h, 0, 0)),
                  vec, vec, vec],
        out_specs=pl.BlockSpec((tq, d), lambda b, h, i: (b * nq + i, h)),
        scratch_shapes=[pltpu.VMEM((seq, 2 * d), BF16),
                        pltpu.VMEM((seq // tk, FOX_VROWS, tk), BF16),
                        pltpu.VMEM((tq, 2 * d), BF16),
                        pltpu.VMEM((1, tq), F32),
                        pltpu.VMEM((FOX_VROWS, tq), F32),
                        pltpu.VMEM((2, tk, tq), F32)],
        compiler_params=pltpu.CompilerParams(
            dimension_semantics=("arbitrary", "arbitrary", "arbitrary"),
            vmem_limit_bytes=VMEM_LIMIT_V7X),
        name="fox_attn",
    )(proj, proj, proj, proj, f3, wq.reshape(1, d), wk.reshape(1, d), wo.reshape(1, d))


def _conv_silu(buf, src_ref, w_ref, bias, tile):
    assert CONV_WIDTH == 4 and CONV_HALO == 8
    h0 = CONV_HALO
    buf[h0:h0 + tile, :] = src_ref[...].astype(F32)
    x = buf[...]
    z = pltpu.roll(x, 1, 0)
    u = w_ref[1:2, :] * x + w_ref[0:1, :] * z
    acc = w_ref[3:4, :] * x[h0:] + w_ref[2:3, :] * z[h0:] + pltpu.roll(u, 2, 0)[h0:]
    if bias is not None:
        acc = acc + bias
    buf[0:h0, :] = buf[tile:tile + h0, :]
    return _silu(acc)


def _ssd_kernel(z_ref, xs_ref, b_ref, c_ref, g_ref, cwx_ref, cwb_ref, cwc_ref,
                cbx_ref, cbb_ref, cbc_ref, dtb_ref, alog_ref, dexp_ref, nw_ref, e_ref,
                o_ref, bufx, bufb, bufc, ht_ref, *, tile, heads):
    L = CHUNK
    P = SSD_HEAD_DIM
    N = SSD_STATE
    per_group = heads // SSD_GROUPS
    gw = per_group * P
    nchunk = tile // L

    @pl.when(pl.program_id(1) == 0)
    def _():
        bufx[0:CONV_HALO, :] = jnp.zeros((CONV_HALO, bufx.shape[1]), F32)
        bufb[0:CONV_HALO, :] = jnp.zeros((CONV_HALO, bufb.shape[1]), F32)
        bufc[0:CONV_HALO, :] = jnp.zeros((CONV_HALO, bufc.shape[1]), F32)
        ht_ref[...] = jnp.zeros(ht_ref.shape, F32)

    xs = _conv_silu(bufx, xs_ref, cwx_ref, cbx_ref[...], tile)
    bm = _conv_silu(bufb, b_ref, cwb_ref, cbb_ref[...], tile)
    cm = _conv_silu(bufc, c_ref, cwc_ref, cbc_ref[...], tile)
    bm_b = bm.astype(BF16)
    cm_b = cm.astype(BF16)

    dt_row = _softplus(g_ref[...] + dtb_ref[...])
    a_row = dt_row * (-jnp.exp(alog_ref[...]))

    r, c = _iota2((L, L))
    tril = r >= c
    tri_u = (r <= c).astype(BF16)
    tri_l = tril.astype(BF16)
    eye = (r == c).astype(BF16)
    expand = e_ref[...]
    first_half = lax.broadcasted_iota(jnp.int32, (L, 2 * P), 1) < P

    chunks = []
    for ci in range(nchunk):
        sl = slice(ci * L, (ci + 1) * L)
        a_c = a_row[:, sl]
        cs_row = _dot3(a_c, tri_u)
        cs_col = _dot3_nt(tri_l, a_c)
        dt_col = _dot3_nt(eye, dt_row[:, sl])
        total = cs_col[L - 1:L, :]
        ds_col = jnp.exp(total - cs_col)
        ecs_col = jnp.exp(cs_col)
        cd_row = jnp.broadcast_to(jnp.exp(total), (8, total.shape[1]))
        stacked = jnp.concatenate([dt_col, ds_col, ecs_col, cd_row], axis=0)
        s1 = stacked.astype(BF16)
        s2 = (stacked - s1.astype(F32)).astype(BF16)
        ex = _dot(s1, expand) + _dot(s2, expand)
        xs_c = xs[sl]
        xdt = xs_c * ex[0:L]
        xdt_b = xdt.astype(BF16)
        cbs = [_dot_nt(cm_b[sl, g * N:(g + 1) * N], bm_b[sl, g * N:(g + 1) * N])
               for g in range(SSD_GROUPS)]
        y_pairs = []
        for hp in range(heads // 2):
            xpair = xdt_b[:, hp * 2 * P:(hp + 1) * 2 * P]
            acc = None
            for hd, keep in ((2 * hp, first_half), (2 * hp + 1, ~first_half)):
                seg = cs_col[:, hd:hd + 1] - cs_row[hd:hd + 1, :]
                lm = jnp.exp(jnp.where(tril, seg, NEG_BIG))
                part = _dot((cbs[hd // per_group] * lm).astype(BF16),
                            jnp.where(keep, xpair, jnp.zeros_like(xpair)))
                acc = part if acc is None else acc + part
            y_pairs.append(acc)
        chunks.append(dict(
            y_diag=jnp.concatenate(y_pairs, axis=1),
            xw=(xdt * ex[L:2 * L]).astype(BF16),
            ecs=ex[2 * L:3 * L], cd=ex[3 * L:3 * L + 1], xs=xs_c))

    hts = [ht_ref[g] for g in range(SSD_GROUPS)]
    rows_out = []
    for ci in range(nchunk):
        sl = slice(ci * L, (ci + 1) * L)
        ck = chunks[ci]
        y_off = []
        for g in range(SSD_GROUPS):
            cols = slice(g * gw, (g + 1) * gw)
            y_off.append(_dot(cm_b[sl, g * N:(g + 1) * N], hts[g].astype(BF16)) * ck["ecs"][:, cols])
            hts[g] = hts[g] * ck["cd"][:, cols] + _dot_tn(bm_b[sl, g * N:(g + 1) * N], ck["xw"][:, cols])
        y = ck["y_diag"] + jnp.concatenate(y_off, axis=1)
        y = (y + ck["xs"] * dexp_ref[...]) * _silu(z_ref[sl, :].astype(F32))
        outs = []
        for g in range(SSD_GROUPS):
            yg = y[:, g * gw:(g + 1) * gw]
            ms = jnp.mean(yg * yg, axis=-1, keepdims=True)
            outs.append(yg * lax.rsqrt(ms + EPS))
        rows_out.append(jnp.concatenate(outs, axis=1) * nw_ref[...])
    for g in range(SSD_GROUPS):
        ht_ref[g] = hts[g]
    o_all = jnp.concatenate(rows_out, axis=0) if nchunk > 1 else rows_out[0]
    o_ref[...] = o_all.astype(o_ref.dtype)


def _ssd(proj, gt, conv_w, conv_b, dt_bias, a_log, d_skip, norm_w, bsz, seq, heads,
         z_blk, xs_blk, b_blk, c_blk):
    width = heads * SSD_HEAD_DIM
    bcw = SSD_GROUPS * SSD_STATE
    tile = _pick_tile(seq, (512, 256, 128, 64))
    nt = seq // tile
    rows = SSD_GATE_ROWS
    pad = rows - heads

    def col(v):
        return jnp.pad(v.astype(F32), (0, pad)).reshape(rows, 1)

    head_of_col = jnp.arange(width) // SSD_HEAD_DIM
    expand = (jnp.arange(rows)[:, None] == head_of_col[None, :]).astype(BF16)
    d_exp = jnp.repeat(d_skip.astype(F32), SSD_HEAD_DIM).reshape(1, width)
    cwx, cwb, cwc = conv_w[:, :width], conv_w[:, width:width + bcw], conv_w[:, width + bcw:]
    cbx = conv_b[:width].reshape(1, width)
    cbb = conv_b[width:width + bcw].reshape(1, bcw)
    cbc = conv_b[width + bcw:].reshape(1, bcw)

    def full(shape):
        return pl.BlockSpec(shape, lambda b, t: (0,) * len(shape))

    kern = functools.partial(_ssd_kernel, tile=tile, heads=heads)
    return pl.pallas_call(
        kern,
        out_shape=jax.ShapeDtypeStruct((bsz * seq, width), BF16),
        grid=(bsz, nt),
        in_specs=[pl.BlockSpec((tile, width), lambda b, t: (b * nt + t, z_blk)),
                  pl.BlockSpec((tile, width), lambda b, t: (b * nt + t, xs_blk)),
                  pl.BlockSpec((tile, bcw), lambda b, t: (b * nt + t, b_blk)),
                  pl.BlockSpec((tile, bcw), lambda b, t: (b * nt + t, c_blk)),
                  pl.BlockSpec((rows, tile), lambda b, t: (0, b * nt + t)),
                  full((CONV_WIDTH, width)), full((CONV_WIDTH, bcw)), full((CONV_WIDTH, bcw)),
                  full((1, width)), full((1, bcw)), full((1, bcw)),
                  full((rows, 1)), full((rows, 1)), full((1, width)), full((1, width)),
                  full((rows, width))],
        out_specs=pl.BlockSpec((tile, width), lambda b, t: (b * nt + t, 0)),
        scratch_shapes=[pltpu.VMEM((tile + CONV_HALO, width), F32),
                        pltpu.VMEM((tile + CONV_HALO, bcw), F32),
                        pltpu.VMEM((tile + CONV_HALO, bcw), F32),
                        pltpu.VMEM((SSD_GROUPS, SSD_STATE, width // SSD_GROUPS), F32)],
        compiler_params=pltpu.CompilerParams(dimension_semantics=("arbitrary", "arbitrary"),
                                             vmem_limit_bytes=VMEM_LIMIT_V7X),
        name="ssd",
    )(proj, proj, proj, proj, gt, cwx, cwb, cwc, cbx, cbb, cbc, col(dt_bias), col(a_log),
      d_exp, norm_w.reshape(1, width), expand)


def _unit_lower_inverses(n_list, r, c):
    eye = (r == c).astype(F32)
    same2 = (r >> 1) == (c >> 1)
    ts = [eye - jnp.where(same2, n, 0.0) for n in n_list]
    s = 2
    while s < CHUNK:
        sh = s.bit_length()
        in_block = (r >> sh) == (c >> sh)
        low_left = in_block & ((r & (2 * s - 1)) >= s) & ((c & (2 * s - 1)) < s)
        tbs = [t.astype(BF16) for t in ts]
        xs = [_dot(jnp.where(low_left, n, 0.0).astype(BF16), tb) for n, tb in zip(n_list, tbs)]
        ts = [t - _dot(tb, x.astype(BF16)) for t, tb, x in zip(ts, tbs, xs)]
        s *= 2
    return ts


def _gdn_kernel(*refs, tile, hg, nb, scale):
    q_ref, k_ref, v_ref, z_ref = refs[0:4]
    g_refs = refs[4:4 + nb]
    cwq_ref, cwk_ref, cwv_ref, dtb_ref, alog_ref, nw_ref = refs[4 + nb:10 + nb]
    o_ref, bufq, bufk, bufv, s_ref = refs[10 + nb:]
    L = CHUNK
    D = HEAD_DIM
    nchunk = tile // L

    @pl.when(pl.program_id(2) == 0)
    def _():
        for buf in (bufq, bufk, bufv):
            buf[:, 0:CONV_HALO, :] = jnp.zeros((nb, CONV_HALO, buf.shape[2]), F32)
        s_ref[...] = jnp.zeros(s_ref.shape, F32)

    r, c = _iota2((L, L))
    incl = r >= c
    strict = r > c
    tri_u = (r <= c).astype(BF16)
    tri_eye = jnp.concatenate([incl.astype(BF16), (r == c).astype(BF16)], axis=0)

    cols_c, rows_c, qn, kn, vh = {}, {}, {}, {}, {}
    for bb in range(nb):
        q_all = _conv_silu(bufq.at[bb], q_ref.at[bb], cwq_ref, None, tile)
        k_all = _conv_silu(bufk.at[bb], k_ref.at[bb], cwk_ref, None, tile)
        v_all = _conv_silu(bufv.at[bb], v_ref.at[bb], cwv_ref, None, tile)

        gl = g_refs[bb][...]
        rows8 = lax.broadcasted_iota(jnp.int32, gl.shape, 0)
        decay_log = -jnp.exp(alog_ref[0]) * _softplus(gl + dtb_ref[0])
        gates = jnp.where(rows8 < 4, decay_log, jax.nn.sigmoid(gl))
        gates = jnp.concatenate([gates, jnp.zeros_like(gates)], axis=0)
        for ci in range(nchunk):
            cols_c[(bb, ci)] = _dot3_nt(tri_eye, gates[:, ci * L:(ci + 1) * L])
            rows_c[(bb, ci)] = _dot3(gates[:, ci * L:(ci + 1) * L], tri_u)

        for h in range(hg):
            hs = slice(h * D, (h + 1) * D)
            qh = q_all[:, hs]
            kh = k_all[:, hs]
            qn[(bb, h)] = qh * (lax.rsqrt(jnp.sum(qh * qh, axis=-1, keepdims=True) + EPS) * scale)
            kn[(bb, h)] = kh * lax.rsqrt(jnp.sum(kh * kh, axis=-1, keepdims=True) + EPS)
            vh[(bb, h)] = v_all[:, hs]

    streams = [(bb, h) for bb in range(nb) for h in range(hg)]
    units = [(st, ci) for ci in range(nchunk) for st in streams]
    loc = {}
    n_list = []
    for (st, ci) in units:
        bb, h = st
        sl = slice(ci * L, (ci + 1) * L)
        gcc = cols_c[(bb, ci)][0:L, h:h + 1]
        gcr = rows_c[(bb, ci)][h:h + 1, :]
        beta = cols_c[(bb, ci)][L:2 * L, 4 + h:5 + h]
        g_last = gcc[L - 1:L, :]
        decay = jnp.exp(jnp.where(incl, gcc - gcr, NEG_BIG))
        kc = kn[st][sl]
        qc = qn[st][sl]
        kc_b = kc.astype(BF16)
        k_beta = kc * beta
        kk = _dot_nt(k_beta.astype(BF16), kc_b) * decay
        n_list.append(jnp.where(strict, kk, 0.0))
        rhs = jnp.concatenate([vh[st][sl] * beta, k_beta * jnp.exp(gcc)], axis=1)
        loc[(st, ci)] = dict(
            rhs=rhs.astype(BF16),
            qk=(_dot_nt(qc.astype(BF16), kc_b) * decay).astype(BF16),
            k_tail=(kc * jnp.exp(g_last - gcc)).astype(BF16),
            q_dec=(qc * jnp.exp(gcc)).astype(BF16),
            e_last=jnp.exp(g_last))
    t_invs = _unit_lower_inverses(n_list, r, c)
    for unit, t_inv in zip(units, t_invs):
        loc[unit]["sol"] = _dot(t_inv.astype(BF16), loc[unit]["rhs"])

    states = {st: s_ref[st[0], st[1]] for st in streams}
    outs = {}
    for ci in range(nchunk):
        sb = {st: states[st].astype(BF16) for st in streams}
        v_new = {}
        for st in streams:
            sol = loc[(st, ci)]["sol"]
            v_new[st] = (sol[:, 0:D] - _dot(sol[:, D:2 * D].astype(BF16), sb[st])).astype(BF16)
        for st in streams:
            u = loc[(st, ci)]
            outs[(st, ci)] = _dot(u["q_dec"], sb[st]) + _dot(u["qk"], v_new[st])
            states[st] = states[st] * u["e_last"] + _dot_tn(u["k_tail"], v_new[st])
    for st in streams:
        s_ref[st[0], st[1]] = states[st]

    for bb in range(nb):
        rows = []
        for ci in range(nchunk):
            heads_out = []
            for h in range(hg):
                o = outs[((bb, h), ci)]
                ms = jnp.mean(o * o, axis=-1, keepdims=True)
                heads_out.append(o * lax.rsqrt(ms + EPS) * nw_ref[...])
            rows.append(jnp.concatenate(heads_out, axis=1))
        o_all = jnp.concatenate(rows, axis=0) if nchunk > 1 else rows[0]
        o_ref[bb] = (o_all * _silu(z_ref[bb].astype(F32))).astype(o_ref.dtype)


def _gdn(proj, gt, conv_w, dt_bias, a_log, norm_w, bsz, seq, heads, hg, q_blk, k_blk, v_blk,
         z_blk, gate_blk):
    d = HEAD_DIM
    width = heads * d
    gwid = hg * d
    ngrp = heads // hg
    nb = 2 if bsz % 2 == 0 else 1
    tile = _pick_tile(seq, (512 // nb, 128, 64))
    nt = seq // tile
    cwq, cwk, cwv = conv_w[:, :width], conv_w[:, width:2 * width], conv_w[:, 2 * width:]
    proj3 = proj.reshape(bsz, seq, proj.shape[1])

    def grp_col(v):
        v = jnp.pad(v.astype(F32).reshape(ngrp, hg), ((0, 0), (0, GDN_GATE_ROWS - hg)))
        return v.reshape(ngrp, GDN_GATE_ROWS, 1)

    def tok(blk):
        return pl.BlockSpec((nb, tile, gwid), lambda b, g, t: (b, t, blk + g))

    def gate(bb):
        return pl.BlockSpec((GDN_GATE_ROWS, tile),
                            lambda b, g, t: (gate_blk + g, (b * nb + bb) * nt + t))

    def cw():
        return pl.BlockSpec((CONV_WIDTH, gwid), lambda b, g, t: (0, g))

    def gcol():
        return pl.BlockSpec((1, GDN_GATE_ROWS, 1), lambda b, g, t: (g, 0, 0))

    kern = functools.partial(_gdn_kernel, tile=tile, hg=hg, nb=nb, scale=d ** -0.5)
    out = pl.pallas_call(
        kern,
        out_shape=jax.ShapeDtypeStruct((bsz, seq, width), BF16),
        grid=(bsz // nb, ngrp, nt),
        in_specs=[tok(q_blk), tok(k_blk), tok(v_blk), tok(z_blk)]
                 + [gate(bb) for bb in range(nb)]
                 + [cw(), cw(), cw(), gcol(), gcol(), pl.BlockSpec((1, d), lambda b, g, t: (0, 0))],
        out_specs=pl.BlockSpec((nb, tile, gwid), lambda b, g, t: (b, t, g)),
        scratch_shapes=[pltpu.VMEM((nb, tile + CONV_HALO, gwid), F32),
                        pltpu.VMEM((nb, tile + CONV_HALO, gwid), F32),
                        pltpu.VMEM((nb, tile + CONV_HALO, gwid), F32),
                        pltpu.VMEM((nb, hg, d, d), F32)],
        compiler_params=pltpu.CompilerParams(
            dimension_semantics=("arbitrary", "arbitrary", "arbitrary"),
            vmem_limit_bytes=VMEM_LIMIT_V7X),
        name="gdn",
    )(proj3, proj3, proj3, proj3, *([gt] * nb), cwq, cwk, cwv, grp_col(dt_bias), grp_col(a_log),
      norm_w.reshape(1, d))
    return out.reshape(bsz * seq, width)


def _out_proj_kernel(x_ref, mf_ref, ms_ref, mg_ref, w_ref, o_ref):
    r1 = mf_ref.shape[1]
    r2 = r1 + ms_ref.shape[1]
    acc = _dot(mf_ref[...], w_ref[0, 0:r1, :])
    acc = acc + _dot(ms_ref[...], w_ref[0, r1:r2, :])
    acc = acc + _dot(mg_ref[...], w_ref[0, r2:, :])
    o_ref[...] = x_ref[...] + acc


def _out_proj(x2d, mf, ms, mg, w_all, layer):
    m, d = x2d.shape
    kdim = w_all.shape[1]
    tm = _pick_tile(m, (1024, 512, 256, 128))
    tn = _pick_tile(d, (1024, 512, 256, 128))

    def act(a):
        return pl.BlockSpec((tm, a.shape[1]), lambda i, j: (i, 0))

    return pl.pallas_call(
        _out_proj_kernel,
        out_shape=jax.ShapeDtypeStruct((m, d), F32),
        grid=(m // tm, d // tn),
        in_specs=[pl.BlockSpec((tm, tn), lambda i, j: (i, j)),
                  act(mf), act(ms), act(mg),
                  pl.BlockSpec((1, kdim, tn), lambda i, j: (layer, 0, j))],
        out_specs=pl.BlockSpec((tm, tn), lambda i, j: (i, j)),
        compiler_params=pltpu.CompilerParams(dimension_semantics=("arbitrary", "arbitrary"),
                                             vmem_limit_bytes=VMEM_LIMIT_V7X),
        name="out_proj",
    )(x2d, mf, ms, mg, w_all)


def _layer(x2d, bsz, seq, layer, norm_w, w_in, w_out_all, fox_b_f, fox_q_norm_w, fox_k_norm_w,
           fox_out_norm_w, ssd_conv_w, ssd_conv_b, ssd_dt_bias, ssd_a_log, ssd_d, ssd_norm_w,
           gdn_conv_w, gdn_dt_bias, gdn_a_log, gdn_norm_w):
    d_model = x2d.shape[1]
    fox_w = d_model // 4
    ssd_w = 3 * d_model // 8
    gdn_w = d_model - fox_w - ssd_w
    fox_h = fox_w // HEAD_DIM
    ssd_h = ssd_w // SSD_HEAD_DIM
    gdn_h = gdn_w // HEAD_DIM
    bcw = SSD_GROUPS * SSD_STATE
    hg = 4 if gdn_h % 4 == 0 else 2
    ngrp = gdn_h // hg
    assert fox_h <= FOX_GATE_ROWS and ssd_h <= SSD_GATE_ROWS and ssd_h % SSD_GROUPS == 0
    assert gdn_h % hg == 0 and seq % CHUNK == 0

    sizes = (3 * fox_w, fox_h, fox_w, ssd_w + 2 * bcw, ssd_w, ssd_h, 3 * gdn_w, gdn_w, gdn_h, gdn_h)
    offs = [0]
    for s in sizes:
        offs.append(offs[-1] + s)
    (o_fqkv, o_ff, o_fz, o_sxbc, o_sz, o_sdt, o_gqkv, o_gz, o_gb, o_ga) = offs[:-1]

    def cols(o, n):
        return w_in[:, o:o + n].astype(BF16)

    w_main = jnp.concatenate(
        [cols(o_sz, ssd_w), cols(o_sxbc, ssd_w + 2 * bcw), cols(o_fqkv, 3 * fox_w),
         cols(o_fz, fox_w), cols(o_gqkv, 3 * gdn_w), cols(o_gz, gdn_w)], axis=1)
    c_sz, c_sxs, c_sb = 0, ssd_w, 2 * ssd_w
    c_sc = c_sb + bcw
    c_fq = c_sc + bcw
    c_fz = c_fq + 3 * fox_w
    c_gq = c_fz + fox_w
    c_gz = c_gq + 3 * gdn_w
    gwid = hg * HEAD_DIM
    assert c_sb % bcw == 0 and c_fq % HEAD_DIM == 0 and c_gq % gwid == 0 and gdn_w % gwid == 0

    def gcols(o, n, total):
        return jnp.pad(w_in[:, o:o + n].astype(BF16), ((0, 0), (0, total - n)))

    gate_cols = [gcols(o_sdt, ssd_h, SSD_GATE_ROWS), gcols(o_ff, fox_h, FOX_GATE_ROWS)]
    for g in range(ngrp):
        gate_cols.append(gcols(o_ga + g * hg, hg, 4))
        gate_cols.append(gcols(o_gb + g * hg, hg, 4))
    n_gate_rows = SSD_GATE_ROWS + FOX_GATE_ROWS + GDN_GATE_ROWS * ngrp
    gate_cols.append(jnp.zeros((d_model, -n_gate_rows % 128), BF16))
    wg = jnp.concatenate(gate_cols, axis=1)
    fox_row_blk = SSD_GATE_ROWS // FOX_GATE_ROWS
    gdn_row_blk = (SSD_GATE_ROWS + FOX_GATE_ROWS) // GDN_GATE_ROWS

    h, gt = _prenorm(x2d, norm_w, wg, n_gate_rows)
    proj = _in_proj(h, w_main)

    b_f = jnp.pad(fox_b_f.astype(F32), (0, FOX_GATE_ROWS - fox_h)).reshape(FOX_GATE_ROWS, 1)
    f_cum = _fox_gate(gt, b_f, bsz, seq, fox_row_blk)
    f3 = f_cum.reshape(bsz * FOX_GATE_ROWS, 1, seq)
    mix_fox = _fox_attn(proj, f3, fox_q_norm_w, fox_k_norm_w, fox_out_norm_w, bsz, seq, fox_h,
                        c_fq // HEAD_DIM, (c_fq + fox_w) // HEAD_DIM,
                        (c_fq + 2 * fox_w) // HEAD_DIM, c_fz // HEAD_DIM)
    mix_ssd = _ssd(proj, gt, ssd_conv_w, ssd_conv_b, ssd_dt_bias, ssd_a_log, ssd_d, ssd_norm_w,
                   bsz, seq, ssd_h, c_sz // ssd_w, c_sxs // ssd_w, c_sb // bcw, c_sc // bcw)
    mix_gdn = _gdn(proj, gt, gdn_conv_w, gdn_dt_bias, gdn_a_log, gdn_norm_w, bsz, seq, gdn_h, hg,
                   c_gq // gwid, (c_gq + gdn_w) // gwid, (c_gq + 2 * gdn_w) // gwid,
                   c_gz // gwid, gdn_row_blk)

    return _out_proj(x2d, mix_fox, mix_ssd, mix_gdn, w_out_all, layer)


def kernel(x, norm_w, w_in, w_out, fox_b_f, fox_q_norm_w, fox_k_norm_w, fox_out_norm_w, ssd_conv_w, ssd_conv_b, ssd_dt_bias, ssd_A_log, ssd_D, ssd_norm_w, gdn_conv_w, gdn_dt_bias, gdn_A_log, gdn_norm_w):
    bsz, seq, d_model = x.shape
    x2d = x.reshape(bsz * seq, d_model)
    w_out_b = w_out.astype(BF16)
    for l in range(norm_w.shape[0]):
        x2d = _layer(x2d, bsz, seq, l, norm_w[l], w_in[l], w_out_b, fox_b_f[l], fox_q_norm_w[l],
                     fox_k_norm_w[l], fox_out_norm_w[l], ssd_conv_w[l], ssd_conv_b[l],
                     ssd_dt_bias[l], ssd_A_log[l], ssd_D[l], ssd_norm_w[l], gdn_conv_w[l],
                     gdn_dt_bias[l], gdn_A_log[l], gdn_norm_w[l])
    return x2d.reshape(bsz, seq, d_model)
```

```python
import functools

import jax
import jax.numpy as jnp
from jax import lax
from jax.experimental import pallas as pl
from jax.experimental.pallas import tpu as pltpu

F32 = jnp.float32
BF16 = jnp.bfloat16

EPS = 1e-6
CHUNK = 64
HEAD_DIM = 128
SSD_HEAD_DIM = 64
SSD_GROUPS = 4
SSD_STATE = 128
CONV_WIDTH = 4
CONV_HALO = 8
SSD_GATE_ROWS = 32
FOX_GATE_ROWS = 8
GDN_GATE_ROWS = 8
NEG_BIG = -1e30
VMEM_LIMIT_V7X = 56 * 1024 * 1024


def _dot(a, b):
    return jnp.dot(a, b, preferred_element_type=F32)


def _dot_nt(a, b):
    return lax.dot_general(a, b, (((1,), (1,)), ((), ())), preferred_element_type=F32)


def _dot_tn(a, b):
    return lax.dot_general(a, b, (((0,), (0,)), ((), ())), preferred_element_type=F32)


def _split3(a):
    a1 = a.astype(BF16)
    r1 = a - a1.astype(F32)
    a2 = r1.astype(BF16)
    r2 = r1 - a2.astype(F32)
    return a1, a2, r2.astype(BF16)


def _dot3(a, m):
    p1, p2, p3 = _split3(a)
    return _dot(p1, m) + _dot(p2, m) + _dot(p3, m)


def _dot3_nt(m, a):
    p1, p2, p3 = _split3(a)
    return _dot_nt(m, p1) + _dot_nt(m, p2) + _dot_nt(m, p3)


def _silu(x):
    return x * jax.nn.sigmoid(x)


def _softplus(x):
    return jnp.maximum(x, 0.0) + jnp.log1p(jnp.exp(-jnp.abs(x)))


def _log_sigmoid(x):
    return jnp.minimum(x, 0.0) - jnp.log1p(jnp.exp(-jnp.abs(x)))


def _iota2(shape):
    return (lax.broadcasted_iota(jnp.int32, shape, 0), lax.broadcasted_iota(jnp.int32, shape, 1))


def _pick_tile(n, candidates):
    for c in candidates:
        if n % c == 0 and n // c >= 2:
            return c
    return n


def _prenorm_kernel(x_ref, nw_ref, wg_ref, h_ref, gt_ref, wgt_ref):
    @pl.when(pl.program_id(0) == 0)
    def _():
        wgt_ref[...] = jnp.transpose(wg_ref[...].astype(F32)).astype(BF16)

    x = x_ref[...]
    ms = jnp.mean(x * x, axis=-1, keepdims=True)
    hb = (x * lax.rsqrt(ms + EPS) * nw_ref[...]).astype(BF16)
    h_ref[...] = hb
    gt_ref[...] = _dot_nt(wgt_ref[0:gt_ref.shape[0], :], hb)


def _prenorm(x2d, norm_w, wg, gate_rows):
    m, d = x2d.shape
    gcols = wg.shape[1]
    tm = _pick_tile(m, (512, 256, 128))
    return pl.pallas_call(
        _prenorm_kernel,
        out_shape=(jax.ShapeDtypeStruct((m, d), BF16), jax.ShapeDtypeStruct((gate_rows, m), F32)),
        grid=(m // tm,),
        in_specs=[pl.BlockSpec((tm, d), lambda i: (i, 0)),
                  pl.BlockSpec((1, d), lambda i: (0, 0)),
                  pl.BlockSpec((d, gcols), lambda i: (0, 0))],
        out_specs=(pl.BlockSpec((tm, d), lambda i: (i, 0)),
                   pl.BlockSpec((gate_rows, tm), lambda i: (0, i))),
        scratch_shapes=[pltpu.VMEM((gcols, d), BF16)],
        compiler_params=pltpu.CompilerParams(dimension_semantics=("arbitrary",),
                                             vmem_limit_bytes=VMEM_LIMIT_V7X),
        name="prenorm_gates",
    )(x2d, norm_w.reshape(1, d), wg)


def _in_proj_kernel(h_ref, w_ref, o_ref):
    o_ref[...] = _dot(h_ref[...], w_ref[...]).astype(o_ref.dtype)


def _in_proj(h, w):
    m, d = h.shape
    n = w.shape[1]
    tm = _pick_tile(m, (1024, 512, 256, 128))
    tn = _pick_tile(n, (1024, 512, 256, 128))
    return pl.pallas_call(
        _in_proj_kernel,
        out_shape=jax.ShapeDtypeStruct((m, n), BF16),
        grid=(m // tm, n // tn),
        in_specs=[pl.BlockSpec((tm, d), lambda i, j: (i, 0)),
                  pl.BlockSpec((d, tn), lambda i, j: (0, j))],
        out_specs=pl.BlockSpec((tm, tn), lambda i, j: (i, j)),
        compiler_params=pltpu.CompilerParams(dimension_semantics=("arbitrary", "arbitrary"),
                                             vmem_limit_bytes=VMEM_LIMIT_V7X),
        name="in_proj",
    )(h, w)


def _fox_gate_kernel(g_ref, b_ref, f_ref):
    seq = g_ref.shape[1]
    lf = _log_sigmoid(g_ref[...] + b_ref[...])
    r, c = _iota2((128, 128))
    upper = (r <= c).astype(BF16)
    carry = jnp.zeros((lf.shape[0], 1), F32)
    for i in range(seq // 128):
        cs = _dot3(lf[:, i * 128:(i + 1) * 128], upper) + carry
        f_ref[0, :, i * 128:(i + 1) * 128] = cs
        carry = cs[:, 127:128]


def _fox_gate(gt, b_f, bsz, seq, row_block):
    rows = FOX_GATE_ROWS
    return pl.pallas_call(
        _fox_gate_kernel,
        out_shape=jax.ShapeDtypeStruct((bsz, rows, seq), F32),
        grid=(bsz,),
        in_specs=[pl.BlockSpec((rows, seq), lambda b: (row_block, b)),
                  pl.BlockSpec((rows, 1), lambda b: (0, 0))],
        out_specs=pl.BlockSpec((1, rows, seq), lambda b: (b, 0, 0)),
        compiler_params=pltpu.CompilerParams(dimension_semantics=("arbitrary",)),
        name="fox_gate",
    )(gt, b_f)


LOG2E = 1.4426950408889634
FOX_VROWS = 144


def _fox_attn_kernel(q_ref, k_ref, v_ref, z_ref, f_ref, wq_ref, wk_ref, wo_ref, o_ref,
                     ka_ref, vat_ref, qa_ref, m_ref, acc_ref, s_ref, *, tk, scale):
    qi = pl.program_id(2)
    seq = k_ref.shape[0]
    d = HEAD_DIM
    tq = 2 * tk
    lane = lax.broadcasted_iota(jnp.int32, (tq, d), 1)

    @pl.when(qi == 0)
    def _():
        row8 = lax.broadcasted_iota(jnp.int32, (8, tk), 0)
        ones_row = (lax.broadcasted_iota(jnp.int32, (FOX_VROWS - d, tk), 0) == 0).astype(BF16)

        def body(i, carry):
            r0 = pl.multiple_of(i * tk, tk)
            kk = k_ref[pl.ds(r0, tk), :].astype(F32)
            ms = jnp.mean(kk * kk, axis=-1, keepdims=True)
            ka_ref[pl.ds(r0, tk), 0:d] = (kk * lax.rsqrt(ms + EPS) * wk_ref[...]).astype(BF16)
            frow = f_ref[0, :, pl.ds(r0, tk)]
            b1, b2, b3 = _split3((frow[:, 0:1] - frow) * LOG2E)
            parts = jnp.where(row8 == 0, b1.astype(F32),
                              jnp.where(row8 == 1, b2.astype(F32),
                                        jnp.where(row8 == 2, b3.astype(F32), 0.0)))
            parts = jnp.concatenate([parts, jnp.zeros((d - 8, tk), F32)], axis=0)
            ka_ref[pl.ds(r0, tk), d:2 * d] = jnp.transpose(parts).astype(BF16)
            vt = jnp.transpose(v_ref[pl.ds(r0, tk), :].astype(F32))
            vat_ref[i, 0:d, :] = vt.astype(BF16)
            vat_ref[i, d:FOX_VROWS, :] = ones_row
            return carry
        lax.fori_loop(0, seq // tk, body, 0)

    q = q_ref[...].astype(F32)
    ms = jnp.mean(q * q, axis=-1, keepdims=True)
    qa_ref[:, 0:d] = (q * lax.rsqrt(ms + EPS) * (wq_ref[...] * (scale * LOG2E))).astype(BF16)
    qa_ref[:, d:2 * d] = (lane < 3).astype(BF16)
    q0 = pl.multiple_of(qi * tq, tq)
    f_first = f_ref[0, :, pl.ds(q0, tk)][:, 0:1]

    m_ref[...] = jnp.full(m_ref.shape, NEG_BIG, F32)
    acc_ref[...] = jnp.zeros(acc_ref.shape, F32)

    def scores_to(slot, kb, halves=(0, 1)):
        r0 = pl.multiple_of(kb * tk, tk)
        lo, hi = halves[0] * tk, (halves[-1] + 1) * tk
        s_ref[slot, :, lo:hi] = _dot_nt(ka_ref[pl.ds(r0, tk), :], qa_ref[lo:hi, :])

    def consume(kb, slot, halves=(0, 1), causal_half=None):
        r0 = pl.multiple_of(kb * tk, tk)
        shift = (f_first - f_ref[0, :, pl.ds(r0, tk)][:, 0:1]) * LOG2E
        results = []
        for h in halves:
            cols = slice(h * tk, (h + 1) * tk)
            s = s_ref[slot, :, cols]
            if h == causal_half:
                r, c = _iota2((tk, tk))
                s = jnp.where(r <= c, s, NEG_BIG)
            m_prev = m_ref[:, cols]
            m_new = jnp.maximum(m_prev, jnp.max(s, axis=0, keepdims=True) + shift)
            p = jnp.exp2(s - (m_new - shift)).astype(BF16)
            alpha = jnp.exp2(m_prev - m_new)
            pv = _dot(vat_ref[kb], p)
            results.append((cols, m_new, alpha * acc_ref[:, cols] + pv))
        for cols, m_new, acc_new in results:
            m_ref[:, cols] = m_new
            acc_ref[:, cols] = acc_new

    scores_to(0, 0)

    def pair_body(j, carry):
        kb = 2 * j
        scores_to(1, kb + 1)
        consume(kb, 0)
        scores_to(0, kb + 2)
        consume(kb + 1, 1)
        return carry
    lax.fori_loop(0, qi, pair_body, 0)

    scores_to(1, 2 * qi + 1, halves=(1,))
    consume(2 * qi, 0, causal_half=0)
    consume(2 * qi + 1, 1, halves=(1,), causal_half=1)

    o = jnp.transpose(acc_ref[0:d, :] / acc_ref[d:d + 1, :])
    ms = jnp.mean(o * o, axis=-1, keepdims=True)
    o = o * lax.rsqrt(ms + EPS) * wo_ref[...]
    o_ref[...] = (o * _silu(z_ref[...].astype(F32))).astype(o_ref.dtype)


def _fox_attn(proj, f3, wq, wk, wo, bsz, seq, heads, q_blk, k_blk, v_blk, z_blk):
    d = HEAD_DIM
    tk = _pick_tile(seq // 2, (1024, 512, 256, 128))
    tq = 2 * tk
    nq = seq // tq
    assert seq % tq == 0
    kern = functools.partial(_fox_attn_kernel, tk=tk, scale=d ** -0.5)
    vec = pl.BlockSpec((1, d), lambda b, h, i: (0, 0))
    return pl.pallas_call(
        kern,
        out_shape=jax.ShapeDtypeStruct((bsz * seq, heads * d), BF16),
        grid=(bsz, heads, nq),
        in_specs=[pl.BlockSpec((tq, d), lambda b, h, i: (b * nq + i, q_blk + h)),
                  pl.BlockSpec((seq, d), lambda b, h, i: (b, k_blk + h)),
                  pl.BlockSpec((seq, d), lambda b, h, i: (b, v_blk + h)),
                  pl.BlockSpec((tq, d), lambda b, h, i: (b * nq + i, z_blk + h)),
                  pl.BlockSpec((1, 1, seq), lambda b, h, i: (b * FOX_GATE_ROWS + h, 0, 0)),
                  vec, vec, vec],
        out_specs=pl.BlockSpec((tq, d), lambda b, h, i: (b * nq + i, h)),
        scratch_shapes=[pltpu.VMEM((seq, 2 * d), BF16),
                        pltpu.VMEM((seq // tk, FOX_VROWS, tk), BF16),
                        pltpu.VMEM((tq, 2 * d), BF16),
                        pltpu.VMEM((1, tq), F32),
                        pltpu.VMEM((FOX_VROWS, tq), F32),
                        pltpu.VMEM((2, tk, tq), F32)],
        compiler_params=pltpu.CompilerParams(
            dimension_semantics=("arbitrary", "arbitrary", "arbitrary"),
            vmem_limit_bytes=VMEM_LIMIT_V7X),
        name="fox_attn",
    )(proj, proj, proj, proj, f3, wq.reshape(1, d), wk.reshape(1, d), wo.reshape(1, d))


def _conv_silu(buf, src_ref, w_ref, bias, tile):
    assert CONV_WIDTH == 4 and CONV_HALO == 8
    h0 = CONV_HALO
    buf[h0:h0 + tile, :] = src_ref[...].astype(F32)
    x = buf[...]
    z = pltpu.roll(x, 1, 0)
    u = w_ref[1:2, :] * x + w_ref[0:1, :] * z
    acc = w_ref[3:4, :] * x[h0:] + w_ref[2:3, :] * z[h0:] + pltpu.roll(u, 2, 0)[h0:]
    if bias is not None:
        acc = acc + bias
    buf[0:h0, :] = buf[tile:tile + h0, :]
    return _silu(acc)


def _ssd_kernel(z_ref, xs_ref, b_ref, c_ref, g_ref, cwx_ref, cwb_ref, cwc_ref,
                cbx_ref, cbb_ref, cbc_ref, dtb_ref, alog_ref, dexp_ref, nw_ref, e_ref,
                o_ref, bufx, bufb, bufc, ht_ref, *, tile, heads):
    L = CHUNK
    P = SSD_HEAD_DIM
    N = SSD_STATE
    per_group = heads // SSD_GROUPS
    gw = per_group * P
    nchunk = tile // L

    @pl.when(pl.program_id(1) == 0)
    def _():
        bufx[0:CONV_HALO, :] = jnp.zeros((CONV_HALO, bufx.shape[1]), F32)
        bufb[0:CONV_HALO, :] = jnp.zeros((CONV_HALO, bufb.shape[1]), F32)
        bufc[0:CONV_HALO, :] = jnp.zeros((CONV_HALO, bufc.shape[1]), F32)
        ht_ref[...] = jnp.zeros(ht_ref.shape, F32)

    xs = _conv_silu(bufx, xs_ref, cwx_ref, cbx_ref[...], tile)
    bm = _conv_silu(bufb, b_ref, cwb_ref, cbb_ref[...], tile)
    cm = _conv_silu(bufc, c_ref, cwc_ref, cbc_ref[...], tile)
    bm_b = bm.astype(BF16)
    cm_b = cm.astype(BF16)

    dt_row = _softplus(g_ref[...] + dtb_ref[...])
    a_row = dt_row * (-jnp.exp(alog_ref[...]))

    r, c = _iota2((L, L))
    tril = r >= c
    tri_u = (r <= c).astype(BF16)
    tri_l = tril.astype(BF16)
    eye = (r == c).astype(BF16)
    expand = e_ref[...]
    first_half = lax.broadcasted_iota(jnp.int32, (L, 2 * P), 1) < P

    chunks = []
    for ci in range(nchunk):
        sl = slice(ci * L, (ci + 1) * L)
        a_c = a_row[:, sl]
        cs_row = _dot3(a_c, tri_u)
        cs_col = _dot3_nt(tri_l, a_c)
        dt_col = _dot3_nt(eye, dt_row[:, sl])
        total = cs_col[L - 1:L, :]
        ds_col = jnp.exp(total - cs_col)
        ecs_col = jnp.exp(cs_col)
        cd_row = jnp.broadcast_to(jnp.exp(total), (8, total.shape[1]))
        stacked = jnp.concatenate([dt_col, ds_col, ecs_col, cd_row], axis=0)
        s1 = stacked.astype(BF16)
        s2 = (stacked - s1.astype(F32)).astype(BF16)
        ex = _dot(s1, expand) + _dot(s2, expand)
        xs_c = xs[sl]
        xdt = xs_c * ex[0:L]
        xdt_b = xdt.astype(BF16)
        cbs = [_dot_nt(cm_b[sl, g * N:(g + 1) * N], bm_b[sl, g * N:(g + 1) * N])
               for g in range(SSD_GROUPS)]
        y_pairs = []
        for hp in range(heads // 2):
            xpair = xdt_b[:, hp * 2 * P:(hp + 1) * 2 * P]
            acc = None
            for hd, keep in ((2 * hp, first_half), (2 * hp + 1, ~first_half)):
                seg = cs_col[:, hd:hd + 1] - cs_row[hd:hd + 1, :]
                lm = jnp.exp(jnp.where(tril, seg, NEG_BIG))
                part = _dot((cbs[hd // per_group] * lm).astype(BF16),
                            jnp.where(keep, xpair, jnp.zeros_like(xpair)))
                acc = part if acc is None else acc + part
            y_pairs.append(acc)
        chunks.append(dict(
            y_diag=jnp.concatenate(y_pairs, axis=1),
            xw=(xdt * ex[L:2 * L]).astype(BF16),
            ecs=ex[2 * L:3 * L], cd=ex[3 * L:3 * L + 1], xs=xs_c))

    hts = [ht_ref[g] for g in range(SSD_GROUPS)]
    rows_out = []
    for ci in range(nchunk):
        sl = slice(ci * L, (ci + 1) * L)
        ck = chunks[ci]
        y_off = []
        for g in range(SSD_GROUPS):
            cols = slice(g * gw, (g + 1) * gw)
            y_off.append(_dot(cm_b[sl, g * N:(g + 1) * N], hts[g].astype(BF16)) * ck["ecs"][:, cols])
            hts[g] = hts[g] * ck["cd"][:, cols] + _dot_tn(bm_b[sl, g * N:(g + 1) * N], ck["xw"][:, cols])
        y = ck["y_diag"] + jnp.concatenate(y_off, axis=1)
        y = (y + ck["xs"] * dexp_ref[...]) * _silu(z_ref[sl, :].astype(F32))
        outs = []
        for g in range(SSD_GROUPS):
            yg = y[:, g * gw:(g + 1) * gw]
            ms = jnp.mean(yg * yg, axis=-1, keepdims=True)
            outs.append(yg * lax.rsqrt(ms + EPS))
        rows_out.append(jnp.concatenate(outs, axis=1) * nw_ref[...])
    for g in range(SSD_GROUPS):
        ht_ref[g] = hts[g]
    o_all = jnp.concatenate(rows_out, axis=0) if nchunk > 1 else rows_out[0]
    o_ref[...] = o_all.astype(o_ref.dtype)


def _ssd(proj, gt, conv_w, conv_b, dt_bias, a_log, d_skip, norm_w, bsz, seq, heads,
         z_blk, xs_blk, b_blk, c_blk):
    width = heads * SSD_HEAD_DIM
    bcw = SSD_GROUPS * SSD_STATE
    tile = _pick_tile(seq, (512, 256, 128, 64))
    nt = seq // tile
    rows = SSD_GATE_ROWS
    pad = rows - heads

    def col(v):
        return jnp.pad(v.astype(F32), (0, pad)).reshape(rows, 1)

    head_of_col = jnp.arange(width) // SSD_HEAD_DIM
    expand = (jnp.arange(rows)[:, None] == head_of_col[None, :]).astype(BF16)
    d_exp = jnp.repeat(d_skip.astype(F32), SSD_HEAD_DIM).reshape(1, width)
    cwx, cwb, cwc = conv_w[:, :width], conv_w[:, width:width + bcw], conv_w[:, width + bcw:]
    cbx = conv_b[:width].reshape(1, width)
    cbb = conv_b[width:width + bcw].reshape(1, bcw)
    cbc = conv_b[width + bcw:].reshape(1, bcw)

    def full(shape):
        return pl.BlockSpec(shape, lambda b, t: (0,) * len(shape))

    kern = functools.partial(_ssd_kernel, tile=tile, heads=heads)
    return pl.pallas_call(
        kern,
        out_shape=jax.ShapeDtypeStruct((bsz * seq, width), BF16),
        grid=(bsz, nt),
        in_specs=[pl.BlockSpec((tile, width), lambda b, t: (b * nt + t, z_blk)),
                  pl.BlockSpec((tile, width), lambda b, t: (b * nt + t, xs_blk)),
                  pl.BlockSpec((tile, bcw), lambda b, t: (b * nt + t, b_blk)),
                  pl.BlockSpec((tile, bcw), lambda b, t: (b * nt + t, c_blk)),
                  pl.BlockSpec((rows, tile), lambda b, t: (0, b * nt + t)),
                  full((CONV_WIDTH, width)), full((CONV_WIDTH, bcw)), full((CONV_WIDTH, bcw)),
                  full((1, width)), full((1, bcw)), full((1, bcw)),
                  full((rows, 1)), full((rows, 1)), full((1, width)), full((1, width)),
                  full((rows, width))],
        out_specs=pl.BlockSpec((tile, width), lambda b, t: (b * nt + t, 0)),
        scratch_shapes=[pltpu.VMEM((tile + CONV_HALO, width), F32),
                        pltpu.VMEM((tile + CONV_HALO, bcw), F32),
                        pltpu.VMEM((tile + CONV_HALO, bcw), F32),
                        pltpu.VMEM((SSD_GROUPS, SSD_STATE, width // SSD_GROUPS), F32)],
        compiler_params=pltpu.CompilerParams(dimension_semantics=("arbitrary", "arbitrary"),
                                             vmem_limit_bytes=VMEM_LIMIT_V7X),
        name="ssd",
    )(proj, proj, proj, proj, gt, cwx, cwb, cwc, cbx, cbb, cbc, col(dt_bias), col(a_log),
      d_exp, norm_w.reshape(1, width), expand)


def _unit_lower_inverses(n_list, r, c):
    eye = (r == c).astype(F32)
    same2 = (r >> 1) == (c >> 1)
    ts = [eye - jnp.where(same2, n, 0.0) for n in n_list]
    s = 2
    while s < CHUNK:
        sh = s.bit_length()
        in_block = (r >> sh) == (c >> sh)
        low_left = in_block & ((r & (2 * s - 1)) >= s) & ((c & (2 * s - 1)) < s)
        tbs = [t.astype(BF16) for t in ts]
        xs = [_dot(jnp.where(low_left, n, 0.0).astype(BF16), tb) for n, tb in zip(n_list, tbs)]
        ts = [t - _dot(tb, x.astype(BF16)) for t, tb, x in zip(ts, tbs, xs)]
        s *= 2
    return ts


def _gdn_kernel(*refs, tile, hg, nb, scale):
    q_ref, k_ref, v_ref, z_ref = refs[0:4]
    g_refs = refs[4:4 + nb]
    cwq_ref, cwk_ref, cwv_ref, dtb_ref, alog_ref, nw_ref = refs[4 + nb:10 + nb]
    o_ref, bufq, bufk, bufv, s_ref = refs[10 + nb:]
    L = CHUNK
    D = HEAD_DIM
    nchunk = tile // L

    @pl.when(pl.program_id(2) == 0)
    def _():
        for buf in (bufq, bufk, bufv):
            buf[:, 0:CONV_HALO, :] = jnp.zeros((nb, CONV_HALO, buf.shape[2]), F32)
        s_ref[...] = jnp.zeros(s_ref.shape, F32)

    r, c = _iota2((L, L))
    incl = r >= c
    strict = r > c
    tri_u = (r <= c).astype(BF16)
    tri_eye = jnp.concatenate([incl.astype(BF16), (r == c).astype(BF16)], axis=0)

    cols_c, rows_c, qn, kn, vh = {}, {}, {}, {}, {}
    for bb in range(nb):
        q_all = _conv_silu(bufq.at[bb], q_ref.at[bb], cwq_ref, None, tile)
        k_all = _conv_silu(bufk.at[bb], k_ref.at[bb], cwk_ref, None, tile)
        v_all = _conv_silu(bufv.at[bb], v_ref.at[bb], cwv_ref, None, tile)

        gl = g_refs[bb][...]
        rows8 = lax.broadcasted_iota(jnp.int32, gl.shape, 0)
        decay_log = -jnp.exp(alog_ref[0]) * _softplus(gl + dtb_ref[0])
        gates = jnp.where(rows8 < 4, decay_log, jax.nn.sigmoid(gl))
        gates = jnp.concatenate([gates, jnp.zeros_like(gates)], axis=0)
        for ci in range(nchunk):
            cols_c[(bb, ci)] = _dot3_nt(tri_eye, gates[:, ci * L:(ci + 1) * L])
            rows_c[(bb, ci)] = _dot3(gates[:, ci * L:(ci + 1) * L], tri_u)

        for h in range(hg):
            hs = slice(h * D, (h + 1) * D)
            qh = q_all[:, hs]
            kh = k_all[:, hs]
            qn[(bb, h)] = qh * (lax.rsqrt(jnp.sum(qh * qh, axis=-1, keepdims=True) + EPS) * scale)
            kn[(bb, h)] = kh * lax.rsqrt(jnp.sum(kh * kh, axis=-1, keepdims=True) + EPS)
            vh[(bb, h)] = v_all[:, hs]

    streams = [(bb, h) for bb in range(nb) for h in range(hg)]
    units = [(st, ci) for ci in range(nchunk) for st in streams]
    loc = {}
    n_list = []
    for (st, ci) in units:
        bb, h = st
        sl = slice(ci * L, (ci + 1) * L)
        gcc = cols_c[(bb, ci)][0:L, h:h + 1]
        gcr = rows_c[(bb, ci)][h:h + 1, :]
        beta = cols_c[(bb, ci)][L:2 * L, 4 + h:5 + h]
        g_last = gcc[L - 1:L, :]
        decay = jnp.exp(jnp.where(incl, gcc - gcr, NEG_BIG))
        kc = kn[st][sl]
        qc = qn[st][sl]
        kc_b = kc.astype(BF16)
        k_beta = kc * beta
        kk = _dot_nt(k_beta.astype(BF16), kc_b) * decay
        n_list.append(jnp.where(strict, kk, 0.0))
        rhs = jnp.concatenate([vh[st][sl] * beta, k_beta * jnp.exp(gcc)], axis=1)
        loc[(st, ci)] = dict(
            rhs=rhs.astype(BF16),
            qk=(_dot_nt(qc.astype(BF16), kc_b) * decay).astype(BF16),
            k_tail=(kc * jnp.exp(g_last - gcc)).astype(BF16),
            q_dec=(qc * jnp.exp(gcc)).astype(BF16),
            e_last=jnp.exp(g_last))
    t_invs = _unit_lower_inverses(n_list, r, c)
    for unit, t_inv in zip(units, t_invs):
        loc[unit]["sol"] = _dot(t_inv.astype(BF16), loc[unit]["rhs"])

    states = {st: s_ref[st[0], st[1]] for st in streams}
    outs = {}
    for ci in range(nchunk):
        sb = {st: states[st].astype(BF16) for st in streams}
        v_new = {}
        for st in streams:
            sol = loc[(st, ci)]["sol"]
            v_new[st] = (sol[:, 0:D] - _dot(sol[:, D:2 * D].astype(BF16), sb[st])).astype(BF16)
        for st in streams:
            u = loc[(st, ci)]
            outs[(st, ci)] = _dot(u["q_dec"], sb[st]) + _dot(u["qk"], v_new[st])
            states[st] = states[st] * u["e_last"] + _dot_tn(u["k_tail"], v_new[st])
    for st in streams:
        s_ref[st[0], st[1]] = states[st]

    for bb in range(nb):
        rows = []
        for ci in range(nchunk):
            heads_out = []
            for h in range(hg):
                o = outs[((bb, h), ci)]
                ms = jnp.mean(o * o, axis=-1, keepdims=True)
                heads_out.append(o * lax.rsqrt(ms + EPS) * nw_ref[...])
            rows.append(jnp.concatenate(heads_out, axis=1))
        o_all = jnp.concatenate(rows, axis=0) if nchunk > 1 else rows[0]
        o_ref[bb] = (o_all * _silu(z_ref[bb].astype(F32))).astype(o_ref.dtype)


def _gdn(proj, gt, conv_w, dt_bias, a_log, norm_w, bsz, seq, heads, hg, q_blk, k_blk, v_blk,
         z_blk, gate_blk):
    d = HEAD_DIM
    width = heads * d
    gwid = hg * d
    ngrp = heads // hg
    nb = 2 if bsz % 2 == 0 else 1
    tile = _pick_tile(seq, (512 // nb, 128, 64))
    nt = seq // tile
    cwq, cwk, cwv = conv_w[:, :width], conv_w[:, width:2 * width], conv_w[:, 2 * width:]
    proj3 = proj.reshape(bsz, seq, proj.shape[1])

    def grp_col(v):
        v = jnp.pad(v.astype(F32).reshape(ngrp, hg), ((0, 0), (0, GDN_GATE_ROWS - hg)))
        return v.reshape(ngrp, GDN_GATE_ROWS, 1)

    def tok(blk):
        return pl.BlockSpec((nb, tile, gwid), lambda b, g, t: (b, t, blk + g))

    def gate(bb):
        return pl.BlockSpec((GDN_GATE_ROWS, tile),
                            lambda b, g, t: (gate_blk + g, (b * nb + bb) * nt + t))

    def cw():
        return pl.BlockSpec((CONV_WIDTH, gwid), lambda b, g, t: (0, g))

    def gcol():
        return pl.BlockSpec((1, GDN_GATE_ROWS, 1), lambda b, g, t: (g, 0, 0))

    kern = functools.partial(_gdn_kernel, tile=tile, hg=hg, nb=nb, scale=d ** -0.5)
    out = pl.pallas_call(
        kern,
        out_shape=jax.ShapeDtypeStruct((bsz, seq, width), BF16),
        grid=(bsz // nb, ngrp, nt),
        in_specs=[tok(q_blk), tok(k_blk), tok(v_blk), tok(z_blk)]
                 + [gate(bb) for bb in range(nb)]
                 + [cw(), cw(), cw(), gcol(), gcol(), pl.BlockSpec((1, d), lambda b, g, t: (0, 0))],
        out_specs=pl.BlockSpec((nb, tile, gwid), lambda b, g, t: (b, t, g)),
        scratch_shapes=[pltpu.VMEM((nb, tile + CONV_HALO, gwid), F32),
                        pltpu.VMEM((nb, tile + CONV_HALO, gwid), F32),
                        pltpu.VMEM((nb, tile + CONV_HALO, gwid), F32),
                        pltpu.VMEM((nb, hg, d, d), F32)],
        compiler_params=pltpu.CompilerParams(
            dimension_semantics=("arbitrary", "arbitrary", "arbitrary"),
            vmem_limit_bytes=VMEM_LIMIT_V7X),
        name="gdn",
    )(proj3, proj3, proj3, proj3, *([gt] * nb), cwq, cwk, cwv, grp_col(dt_bias), grp_col(a_log),
      norm_w.reshape(1, d))
    return out.reshape(bsz * seq, width)


def _out_proj_kernel(x_ref, mf_ref, ms_ref, mg_ref, w_ref, o_ref):
    r1 = mf_ref.shape[1]
    r2 = r1 + ms_ref.shape[1]
    acc = _dot(mf_ref[...], w_ref[0, 0:r1, :])
    acc = acc + _dot(ms_ref[...], w_ref[0, r1:r2, :])
    acc = acc + _dot(mg_ref[...], w_ref[0, r2:, :])
    o_ref[...] = x_ref[...] + acc


def _out_proj(x2d, mf, ms, mg, w_all, layer):
    m, d = x2d.shape
    kdim = w_all.shape[1]
    tm = _pick_tile(m, (1024, 512, 256, 128))
    tn = _pick_tile(d, (1024, 512, 256, 128))

    def act(a):
        return pl.BlockSpec((tm, a.shape[1]), lambda i, j: (i, 0))

    return pl.pallas_call(
        _out_proj_kernel,
        out_shape=jax.ShapeDtypeStruct((m, d), F32),
        grid=(m // tm, d // tn),
        in_specs=[pl.BlockSpec((tm, tn), lambda i, j: (i, j)),
                  act(mf), act(ms), act(mg),
                  pl.BlockSpec((1, kdim, tn), lambda i, j: (layer, 0, j))],
        out_specs=pl.BlockSpec((tm, tn), lambda i, j: (i, j)),
        compiler_params=pltpu.CompilerParams(dimension_semantics=("arbitrary", "arbitrary"),
                                             vmem_limit_bytes=VMEM_LIMIT_V7X),
        name="out_proj",
    )(x2d, mf, ms, mg, w_all)


def _layer(x2d, bsz, seq, layer, norm_w, w_in, w_out_all, fox_b_f, fox_q_norm_w, fox_k_norm_w,
           fox_out_norm_w, ssd_conv_w, ssd_conv_b, ssd_dt_bias, ssd_a_log, ssd_d, ssd_norm_w,
           gdn_conv_w, gdn_dt_bias, gdn_a_log, gdn_norm_w):
    d_model = x2d.shape[1]
    fox_w = d_model // 4
    ssd_w = 3 * d_model // 8
    gdn_w = d_model - fox_w - ssd_w
    fox_h = fox_w // HEAD_DIM
    ssd_h = ssd_w // SSD_HEAD_DIM
    gdn_h = gdn_w // HEAD_DIM
    bcw = SSD_GROUPS * SSD_STATE
    hg = 4 if gdn_h % 4 == 0 else 2
    ngrp = gdn_h // hg
    assert fox_h <= FOX_GATE_ROWS and ssd_h <= SSD_GATE_ROWS and ssd_h % SSD_GROUPS == 0
    assert gdn_h % hg == 0 and seq % CHUNK == 0

    sizes = (3 * fox_w, fox_h, fox_w, ssd_w + 2 * bcw, ssd_w, ssd_h, 3 * gdn_w, gdn_w, gdn_h, gdn_h)
    offs = [0]
    for s in sizes:
        offs.append(offs[-1] + s)
    (o_fqkv, o_ff, o_fz, o_sxbc, o_sz, o_sdt, o_gqkv, o_gz, o_gb, o_ga) = offs[:-1]

    def cols(o, n):
        return w_in[:, o:o + n].astype(BF16)

    w_main = jnp.concatenate(
        [cols(o_sz, ssd_w), cols(o_sxbc, ssd_w + 2 * bcw), cols(o_fqkv, 3 * fox_w),
         cols(o_fz, fox_w), cols(o_gqkv, 3 * gdn_w), cols(o_gz, gdn_w)], axis=1)
    c_sz, c_sxs, c_sb = 0, ssd_w, 2 * ssd_w
    c_sc = c_sb + bcw
    c_fq = c_sc + bcw
    c_fz = c_fq + 3 * fox_w
    c_gq = c_fz + fox_w
    c_gz = c_gq + 3 * gdn_w
    gwid = hg * HEAD_DIM
    assert c_sb % bcw == 0 and c_fq % HEAD_DIM == 0 and c_gq % gwid == 0 and gdn_w % gwid == 0

    def gcols(o, n, total):
        return jnp.pad(w_in[:, o:o + n].astype(BF16), ((0, 0), (0, total - n)))

    gate_cols = [gcols(o_sdt, ssd_h, SSD_GATE_ROWS), gcols(o_ff, fox_h, FOX_GATE_ROWS)]
    for g in range(ngrp):
        gate_cols.append(gcols(o_ga + g * hg, hg, 4))
        gate_cols.append(gcols(o_gb + g * hg, hg, 4))
    n_gate_rows = SSD_GATE_ROWS + FOX_GATE_ROWS + GDN_GATE_ROWS * ngrp
    gate_cols.append(jnp.zeros((d_model, -n_gate_rows % 128), BF16))
    wg = jnp.concatenate(gate_cols, axis=1)
    fox_row_blk = SSD_GATE_ROWS // FOX_GATE_ROWS
    gdn_row_blk = (SSD_GATE_ROWS + FOX_GATE_ROWS) // GDN_GATE_ROWS

    h, gt = _prenorm(x2d, norm_w, wg, n_gate_rows)
    proj = _in_proj(h, w_main)

    b_f = jnp.pad(fox_b_f.astype(F32), (0, FOX_GATE_ROWS - fox_h)).reshape(FOX_GATE_ROWS, 1)
    f_cum = _fox_gate(gt, b_f, bsz, seq, fox_row_blk)
    f3 = f_cum.reshape(bsz * FOX_GATE_ROWS, 1, seq)
    mix_fox = _fox_attn(proj, f3, fox_q_norm_w, fox_k_norm_w, fox_out_norm_w, bsz, seq, fox_h,
                        c_fq // HEAD_DIM, (c_fq + fox_w) // HEAD_DIM,
                        (c_fq + 2 * fox_w) // HEAD_DIM, c_fz // HEAD_DIM)
    mix_ssd = _ssd(proj, gt, ssd_conv_w, ssd_conv_b, ssd_dt_bias, ssd_a_log, ssd_d, ssd_norm_w,
                   bsz, seq, ssd_h, c_sz // ssd_w, c_sxs // ssd_w, c_sb // bcw, c_sc // bcw)
    mix_gdn = _gdn(proj, gt, gdn_conv_w, gdn_dt_bias, gdn_a_log, gdn_norm_w, bsz, seq, gdn_h, hg,
                   c_gq // gwid, (c_gq + gdn_w) // gwid, (c_gq + 2 * gdn_w) // gwid,
                   c_gz // gwid, gdn_row_blk)

    return _out_proj(x2d, mix_fox, mix_ssd, mix_gdn, w_out_all, layer)


def kernel(x, norm_w, w_in, w_out, fox_b_f, fox_q_norm_w, fox_k_norm_w, fox_out_norm_w, ssd_conv_w, ssd_conv_b, ssd_dt_bias, ssd_A_log, ssd_D, ssd_norm_w, gdn_conv_w, gdn_dt_bias, gdn_A_log, gdn_norm_w):
    bsz, seq, d_model = x.shape
    x2d = x.reshape(bsz * seq, d_model)
    w_out_b = w_out.astype(BF16)
    for l in range(norm_w.shape[0]):
        x2d = _layer(x2d, bsz, seq, l, norm_w[l], w_in[l], w_out_b, fox_b_f[l], fox_q_norm_w[l],
                     fox_k_norm_w[l], fox_out_norm_w[l], ssd_conv_w[l], ssd_conv_b[l],
                     ssd_dt_bias[l], ssd_A_log[l], ssd_D[l], ssd_norm_w[l], gdn_conv_w[l],
                     gdn_dt_bias[l], gdn_A_log[l], gdn_norm_w[l])
    return x2d.reshape(bsz, seq, d_model)
```
